```python
import math
import jax
import jax.numpy as jnp
from jax import lax
import numpy as np

D_MODEL = 2048
BATCH = 2
SEQ = 4096
DEPTH = 4
DEC_BATCH = 8
DEC_SEQ = 1
PAST_LEN = 16384
PAGE_SIZE = 128

HEAD_DIM = 128
D_MIX = D_MODEL
ATT_WIDTH = D_MIX // 2
DN_WIDTH = D_MIX - ATT_WIDTH
ATT_HEADS = ATT_WIDTH // HEAD_DIM
ATT_KV_HEADS = ATT_HEADS // 2
ATT_GROUP = ATT_HEADS // ATT_KV_HEADS
KV_WIDTH = ATT_KV_HEADS * HEAD_DIM
MOBA_BLOCK = 256
MOBA_TOPK = 3
MOBA_Q_SWEEP = 32
REL_BUCKETS = 32
REL_MAX_DIST = 2048
DN_HEADS = DN_WIDTH // HEAD_DIM
DN_DK = HEAD_DIM
DN_DV = HEAD_DIM
DN_CHUNK = 64
CONV_W = 4
N_EXPERTS = 16
N_EXPERT_GROUPS = 4
EXPERTS_PER_GROUP = N_EXPERTS // N_EXPERT_GROUPS
MOE_TOPK = 2
D_EXPERT = D_MODEL // 2
MOE_BLOCK = 128
LN_EPS = 1e-5
NORM_EPS = 1e-6
DEEPNORM_ALPHA = (2 * DEPTH) ** 0.25
DEEPNORM_BETA = (8 * DEPTH) ** -0.25
IN_COLS = ATT_WIDTH + 2 * KV_WIDTH + 4 * DN_WIDTH + 2 * DN_HEADS
F32 = jnp.float32

kernel_name = 'hybrid_moba_gated_delta_moe_step'


def layer_norm(x, g, b):
    xf = x.astype(F32)
    mu = jnp.mean(xf, axis=-1, keepdims=True)
    var = jnp.mean(jnp.square(xf - mu), axis=-1, keepdims=True)
    return ((xf - mu) * lax.rsqrt(var + LN_EPS) * g.astype(F32) + b.astype(F32)).astype(x.dtype)


def l2_normalize(x):
    return x * lax.rsqrt(jnp.sum(jnp.square(x), axis=-1, keepdims=True) + NORM_EPS)


def rel_bucket(dist):
    n = jnp.maximum(dist, 0)
    max_exact = REL_BUCKETS // 2
    nf = jnp.maximum(n, 1).astype(F32)
    large = max_exact + (jnp.log(nf / max_exact) / math.log(REL_MAX_DIST / max_exact)
                         * (REL_BUCKETS - max_exact)).astype(jnp.int32)
    large = jnp.minimum(large, REL_BUCKETS - 1)
    return jnp.where(n < max_exact, n, large)


def moba_attend(q, k_all, v_all, q_start, rel_bias):
    b, t = q.shape[0], q.shape[1]
    length = k_all.shape[1]
    n_blk = -(-length // MOBA_BLOCK)
    kpad = ((0, 0), (0, n_blk * MOBA_BLOCK - length), (0, 0), (0, 0))
    kb = jnp.pad(k_all, kpad).reshape(b, n_blk, MOBA_BLOCK, ATT_KV_HEADS, HEAD_DIM)
    vb = jnp.pad(v_all, kpad).reshape(b, n_blk, MOBA_BLOCK, ATT_KV_HEADS, HEAD_DIM)
    head_kv = jnp.arange(ATT_HEADS) // ATT_GROUP
    k_mean = jnp.mean(kb.astype(F32), axis=2)[:, :, head_kv, :]
    qb = min(MOBA_Q_SWEEP, t)
    n_q = -(-t // qb)
    qp = jnp.pad(q, ((0, 0), (0, n_q * qb - t), (0, 0), (0, 0)))
    q_chunks = jnp.moveaxis(qp.reshape(b, n_q, qb, ATT_HEADS, HEAD_DIM), 1, 0)
    pos_chunks = (q_start + jnp.arange(n_q * qb, dtype=jnp.int32)).reshape(n_q, qb)
    k_sel = min(MOBA_TOPK, n_blk)
    scale = HEAD_DIM ** -0.5
    b_ix = jnp.arange(b)[:, None, None, None]
    h_ix = head_kv[None, None, :, None]
    h5 = jnp.arange(ATT_HEADS)[None, None, :, None, None]
    offs = jnp.arange(MOBA_BLOCK, dtype=jnp.int32)
    bias_tab = rel_bias.astype(F32)

    def attend_chunk(args):
        qc, pc = args
        own = jnp.minimum(pc // MOBA_BLOCK, n_blk - 1)
        gate = jnp.einsum('bqhd,bnhd->bqhn', qc.astype(F32), k_mean)
        fully_past = jnp.arange(n_blk)[None, :] < own[:, None]
        gate = jnp.where(fully_past[None, :, None, :], gate, -jnp.inf)
        _, top = lax.top_k(gate, k_sel)
        own_b = jnp.broadcast_to(own[None, :, None, None], (b, qb, ATT_HEADS, 1)).astype(top.dtype)
        blk = jnp.concatenate([top, own_b], axis=-1)
        blk_ok = jnp.concatenate([top < own_b, jnp.ones(own_b.shape, bool)], axis=-1)
        kg = kb[b_ix, blk, :, h_ix, :]
        vg = vb[b_ix, blk, :, h_ix, :]
        logits = jnp.einsum('bqhd,bqhsjd->bqhsj', qc, kg, preferred_element_type=F32) * scale
        kpos = blk[..., None] * MOBA_BLOCK + offs
        dist = pc[None, :, None, None, None] - kpos
        bias = bias_tab[rel_bucket(dist), h5]
        mask = blk_ok[..., None] & (dist >= 0)
        logits = jnp.where(mask, logits + bias, -jnp.inf)
        p = jax.nn.softmax(logits.reshape(b, qb, ATT_HEADS, -1), axis=-1)
        vflat = vg.reshape(b, qb, ATT_HEADS, -1, HEAD_DIM)
        return jnp.einsum('bqhn,bqhnd->bqhd', p.astype(vflat.dtype), vflat)

    out = lax.map(attend_chunk, (q_chunks, pos_chunks))
    return jnp.moveaxis(out, 0, 1).reshape(b, n_q * qb, ATT_HEADS, HEAD_DIM)[:, :t]


def causal_conv(x_new, buf, w):
    t = x_new.shape[1]
    xp = jnp.concatenate([buf.astype(x_new.dtype), x_new], axis=1)
    y = xp[:, 0:t] * w[0]
    for j in range(1, CONV_W):
        y = y + xp[:, j:j + t] * w[j]
    return y, xp[:, t:]


def gated_delta(q, k, v, g, beta, s0):
    b, t, h, dk = q.shape
    dv = v.shape[-1]
    c = min(DN_CHUNK, t)
    n = -(-t // c)
    pad = n * c - t

    def to_chunks(a):
        a = jnp.pad(a.astype(F32), ((0, 0), (0, pad)) + ((0, 0),) * (a.ndim - 2))
        a = a.reshape((b, n, c) + a.shape[2:])
        return jnp.swapaxes(jnp.moveaxis(a, 1, 0), 2, 3)

    qs, ks, vs, gs, bs = (to_chunks(a) for a in (q, k, v, g, beta))
    tri = jnp.tril(jnp.ones((c, c), bool))
    strict = jnp.tril(jnp.ones((c, c), bool), -1)
    eye = jnp.eye(c, dtype=F32)

    def step(s, inp):
        qc, kc, vc, gc, bc = inp
        gcum = jnp.cumsum(gc, axis=-1)
        decay = jnp.exp(jnp.where(tri, gcum[..., :, None] - gcum[..., None, :], -jnp.inf))
        kbeta = kc * bc[..., None]
        a = jnp.where(strict, jnp.einsum('bhid,bhjd->bhij', kbeta, kc) * decay, 0.0)
        rhs = jnp.concatenate([vc * bc[..., None], kbeta * jnp.exp(gcum)[..., None]], axis=-1)
        sol = lax.linalg.triangular_solve(a + eye, rhs, left_side=True, lower=True, unit_diagonal=True)
        u, w = sol[..., :dv], sol[..., dv:]
        v_new = u - jnp.einsum('bhck,bhkv->bhcv', w, s)
        attn = jnp.einsum('bhid,bhjd->bhij', qc, kc) * decay
        o = (jnp.einsum('bhck,bhkv->bhcv', qc * jnp.exp(gcum)[..., None], s)
             + jnp.einsum('bhij,bhjv->bhiv', attn, v_new))
        g_last = gcum[..., -1:]
        s = (s * jnp.exp(g_last)[..., None]
             + jnp.einsum('bhck,bhcv->bhkv', kc * jnp.exp(g_last - gcum)[..., None], v_new))
        return s, o

    s_fin, o = lax.scan(step, s0.astype(F32), (qs, ks, vs, gs, bs))
    o = jnp.moveaxis(jnp.swapaxes(o, 2, 3), 0, 1).reshape(b, n * c, h, dv)[:, :t]
    return o, s_fin


def mixer(x, w_in, w_out, conv_w, a_log, dt_bias, dn_norm_w, rel_bias, k_past, v_past, conv_buf, dn_state, q_start):
    b, t, _ = x.shape
    h = x @ w_in
    o1 = ATT_WIDTH
    o2 = o1 + KV_WIDTH
    o3 = o2 + KV_WIDTH
    o4 = o3 + 3 * DN_WIDTH
    o5 = o4 + DN_WIDTH
    o6 = o5 + DN_HEADS
    q_a = h[..., :o1].reshape(b, t, ATT_HEADS, HEAD_DIM)
    k_a = h[..., o1:o2].reshape(b, t, ATT_KV_HEADS, HEAD_DIM)
    v_a = h[..., o2:o3].reshape(b, t, ATT_KV_HEADS, HEAD_DIM)
    if k_past is None:
        k_all, v_all = k_a, v_a
    else:
        k_all = jnp.concatenate([k_past.astype(k_a.dtype), k_a], axis=1)
        v_all = jnp.concatenate([v_past.astype(v_a.dtype), v_a], axis=1)
    y_a = moba_attend(q_a, k_all, v_all, q_start, rel_bias).reshape(b, t, ATT_WIDTH)
    qkv, conv_new = causal_conv(h[..., o3:o4], conv_buf, conv_w)
    qkv = jax.nn.silu(qkv.astype(F32)).reshape(b, t, 3, DN_HEADS, HEAD_DIM)
    q_b = l2_normalize(qkv[:, :, 0]) * (DN_DK ** -0.5)
    k_b = l2_normalize(qkv[:, :, 1])
    v_b = qkv[:, :, 2]
    beta = jax.nn.sigmoid(h[..., o5:o6].astype(F32))
    g = -jnp.exp(a_log.astype(F32)) * jax.nn.softplus(h[..., o6:].astype(F32) + dt_bias.astype(F32))
    o_b, s_new = gated_delta(q_b, k_b, v_b, g, beta, dn_state)
    o_b = o_b * lax.rsqrt(jnp.mean(jnp.square(o_b), axis=-1, keepdims=True) + NORM_EPS) * dn_norm_w.astype(F32)
    o_b = o_b * jax.nn.silu(h[..., o4:o5].astype(F32)).reshape(b, t, DN_HEADS, DN_DV)
    y = jnp.concatenate([y_a.astype(x.dtype), o_b.reshape(b, t, DN_WIDTH).astype(x.dtype)], axis=-1) @ w_out
    return y, k_a, v_a, s_new.astype(dn_state.dtype), conv_new.astype(conv_buf.dtype)


def gather_pages(pool, page_table):
    rows = pool[page_table]
    return rows.reshape(rows.shape[0], -1, ATT_KV_HEADS, HEAD_DIM)


def expert_dispatch(xf, e_idx, gates, w_gate, w_up, w_down):
    n_tok, d = xf.shape
    n_asg = n_tok * MOE_TOPK
    mb = min(MOE_BLOCK, n_asg)
    n_blocks = -(-n_asg // mb) + N_EXPERTS
    e_flat = e_idx.reshape(-1)
    tok = jnp.arange(n_asg, dtype=jnp.int32) // MOE_TOPK
    order = jnp.argsort(e_flat)
    e_sorted = e_flat[order]
    counts = jnp.zeros((N_EXPERTS,), jnp.int32).at[e_flat].add(1)
    padded = (counts + mb - 1) // mb * mb
    pad_end = jnp.cumsum(padded)
    pad_start = pad_end - padded
    start = jnp.cumsum(counts) - counts
    dest_sorted = pad_start[e_sorted] + jnp.arange(n_asg, dtype=jnp.int32) - start[e_sorted]
    row_tok = jnp.full((n_blocks * mb,), n_tok, jnp.int32).at[dest_sorted].set(tok[order])
    x_rows = jnp.concatenate([xf, jnp.zeros((1, d), xf.dtype)], axis=0)[row_tok].reshape(n_blocks, mb, d)
    blk_start = jnp.arange(n_blocks, dtype=jnp.int32) * mb
    blk_e = jnp.minimum(jnp.sum(pad_end[None, :] <= blk_start[:, None], axis=1), N_EXPERTS - 1)

    def run_block(args):
        xb, e = args
        hid = jax.nn.silu(xb @ w_gate[e]) * (xb @ w_up[e])
        return hid @ w_down[e]

    y_rows = lax.map(run_block, (x_rows, blk_e)).reshape(n_blocks * mb, d)
    dest = jnp.zeros((n_asg,), jnp.int32).at[order].set(dest_sorted)
    y = y_rows[dest].reshape(n_tok, MOE_TOPK, d)
    return jnp.sum(y * gates[..., None].astype(y.dtype), axis=1)


def moe(x, w_router, router_bias, w_gate, w_up, w_down):
    b, t, d = x.shape
    xf = x.reshape(b * t, d)
    n_tok = b * t
    scores = jax.nn.sigmoid(jnp.einsum('nd,de->ne', xf, w_router, preferred_element_type=F32))
    sel = scores + router_bias.astype(F32)
    grp = sel.reshape(n_tok, N_EXPERT_GROUPS, EXPERTS_PER_GROUP)
    grp_score = jnp.sum(lax.top_k(grp, MOE_TOPK)[0], axis=-1)
    _, g_idx = lax.top_k(grp_score, 1)
    in_grp = (jnp.arange(N_EXPERTS)[None, :] // EXPERTS_PER_GROUP) == g_idx
    _, e_idx = lax.top_k(jnp.where(in_grp, sel, -jnp.inf), MOE_TOPK)
    gates = jnp.take_along_axis(scores, e_idx, axis=1)
    gates = gates / jnp.sum(gates, axis=-1, keepdims=True)
    return expert_dispatch(xf, e_idx, gates, w_gate, w_up, w_down).reshape(b, t, d)


def run_group(x, cache_k, cache_v, page_table, conv_state, dn_state, q_start,
              ln_in_g, ln_in_b, rel_bias, w_router, router_bias, w_in, w_out, conv_w, a_log, dt_bias,
              dn_norm_w, ln1_g, ln1_b, ln2_g, ln2_b, w_gate, w_up, w_down):
    x = layer_norm(x, ln_in_g, ln_in_b)
    ks, vs, ss, cs = [], [], [], []
    for l in range(DEPTH):
        if cache_k is None:
            k_past, v_past = None, None
        else:
            k_past = gather_pages(cache_k[l], page_table)
            v_past = gather_pages(cache_v[l], page_table)
        m, k_new, v_new, s_new, c_new = mixer(x, w_in[l], w_out[l], conv_w[l], a_log[l], dt_bias[l], dn_norm_w[l],
                                              rel_bias, k_past, v_past, conv_state[l], dn_state[l], q_start)
        x = layer_norm(DEEPNORM_ALPHA * x + m, ln1_g[l], ln1_b[l])
        f = moe(x, w_router, router_bias, w_gate[l], w_up[l], w_down[l])
        x = layer_norm(DEEPNORM_ALPHA * x + f, ln2_g[l], ln2_b[l])
        ks.append(k_new)
        vs.append(v_new)
        ss.append(s_new)
        cs.append(c_new)
    return x, jnp.stack(ks), jnp.stack(vs), jnp.stack(ss), jnp.stack(cs)


def setup_inputs(seed: int = 0) -> dict:
    key = jax.random.key(seed)
    kk = jax.random.split(key, 40)
    n_pages = PAST_LEN // PAGE_SIZE
    n_used = DEC_BATCH * n_pages
    n_pool = (n_used * 5) // 4

    def nrm(k, shape, s):
        return jax.random.normal(k, shape, F32) * s

    s_in = D_MODEL ** -0.5
    x_prompt = nrm(kk[0], (BATCH, SEQ, D_MODEL), 1.0)
    x_sample = nrm(kk[1], (DEC_BATCH, DEC_SEQ, D_MODEL), 1.0)
    cache_k = nrm(kk[2], (DEPTH, n_pool, PAGE_SIZE, ATT_KV_HEADS, HEAD_DIM), 1.0)
    cache_v = nrm(kk[3], (DEPTH, n_pool, PAGE_SIZE, ATT_KV_HEADS, HEAD_DIM), DEEPNORM_BETA)
    state_dn = nrm(kk[4], (DEPTH, DEC_BATCH, DN_HEADS, DN_DK, DN_DV), 0.05)
    state_conv = nrm(kk[5], (DEPTH, DEC_BATCH, CONV_W - 1, 3 * DN_WIDTH), 1.0)
    page_table = jax.random.permutation(kk[6], n_pool)[:n_used].reshape(DEC_BATCH, n_pages).astype(jnp.int32)
    ln_in_g = 1.0 + nrm(kk[7], (D_MODEL,), 0.02)
    ln_in_b = nrm(kk[8], (D_MODEL,), 0.02)
    rel_bias = nrm(kk[9], (REL_BUCKETS, ATT_HEADS), 0.5)
    w_router = nrm(kk[10], (D_MODEL, N_EXPERTS), s_in)
    router_bias = nrm(kk[11], (N_EXPERTS,), 0.01)
    w_in = jnp.concatenate([
        nrm(kk[12], (DEPTH, D_MODEL, ATT_WIDTH), s_in),
        nrm(kk[13], (DEPTH, D_MODEL, KV_WIDTH), s_in),
        nrm(kk[14], (DEPTH, D_MODEL, KV_WIDTH), s_in * DEEPNORM_BETA),
        nrm(kk[15], (DEPTH, D_MODEL, 2 * DN_WIDTH), s_in),
        nrm(kk[16], (DEPTH, D_MODEL, DN_WIDTH), s_in * DEEPNORM_BETA),
        nrm(kk[17], (DEPTH, D_MODEL, DN_WIDTH), s_in),
        nrm(kk[18], (DEPTH, D_MODEL, 2 * DN_HEADS), s_in),
    ], axis=-1)
    w_out = nrm(kk[19], (DEPTH, D_MIX, D_MODEL), D_MIX ** -0.5 * DEEPNORM_BETA)
    conv_w = nrm(kk[20], (DEPTH, CONV_W, 3 * DN_WIDTH), CONV_W ** -0.5)
    a_log = jnp.log(jax.random.uniform(kk[21], (DEPTH, DN_HEADS), F32, 1.0, 16.0))
    dt = jnp.exp(jax.random.uniform(kk[22], (DEPTH, DN_HEADS), F32, math.log(1e-3), math.log(1e-1)))
    dt_bias = dt + jnp.log(-jnp.expm1(-dt))
    dn_norm_w = 1.0 + nrm(kk[23], (DEPTH, DN_DV), 0.02)
    ln1_g = 1.0 + nrm(kk[24], (DEPTH, D_MODEL), 0.02)
    ln1_b = nrm(kk[25], (DEPTH, D_MODEL), 0.02)
    ln2_g = 1.0 + nrm(kk[26], (DEPTH, D_MODEL), 0.02)
    ln2_b = nrm(kk[27], (DEPTH, D_MODEL), 0.02)
    w_gate = nrm(kk[28], (DEPTH, N_EXPERTS, D_MODEL, D_EXPERT), s_in)
    w_up = nrm(kk[29], (DEPTH, N_EXPERTS, D_MODEL, D_EXPERT), s_in)
    w_down = nrm(kk[30], (DEPTH, N_EXPERTS, D_EXPERT, D_MODEL), D_EXPERT ** -0.5 * DEEPNORM_BETA)
    return {'x_prompt': x_prompt, 'x_sample': x_sample, 'cache_k': cache_k, 'cache_v': cache_v,
            'state_dn': state_dn, 'state_conv': state_conv, 'page_table': page_table,
            'ln_in_g': ln_in_g, 'ln_in_b': ln_in_b, 'rel_bias': rel_bias, 'w_router': w_router,
            'router_bias': router_bias, 'w_in': w_in, 'w_out': w_out, 'conv_w': conv_w, 'a_log': a_log,
            'dt_bias': dt_bias, 'dn_norm_w': dn_norm_w, 'ln1_g': ln1_g, 'ln1_b': ln1_b, 'ln2_g': ln2_g,
            'ln2_b': ln2_b, 'w_gate': w_gate, 'w_up': w_up, 'w_down': w_down}


def reference(x_prompt, x_sample, cache_k, cache_v, state_dn, state_conv, page_table,
              ln_in_g, ln_in_b, rel_bias, w_router, router_bias, w_in, w_out, conv_w, a_log, dt_bias,
              dn_norm_w, ln1_g, ln1_b, ln2_g, ln2_b, w_gate, w_up, w_down):
    weights = (ln_in_g, ln_in_b, rel_bias, w_router, router_bias, w_in, w_out, conv_w, a_log, dt_bias,
               dn_norm_w, ln1_g, ln1_b, ln2_g, ln2_b, w_gate, w_up, w_down)
    n_prompt = x_prompt.shape[0]
    conv0 = jnp.zeros((DEPTH, n_prompt, CONV_W - 1, 3 * DN_WIDTH), state_conv.dtype)
    dn0 = jnp.zeros((DEPTH, n_prompt, DN_HEADS, DN_DK, DN_DV), state_dn.dtype)
    y_prompt, k_prompt, v_prompt, dn_prompt, conv_prompt = run_group(
        x_prompt, None, None, None, conv0, dn0, 0, *weights)
    past_len = page_table.shape[1] * cache_k.shape[2]
    y_sample, k_sample, v_sample, dn_sample, conv_sample = run_group(
        x_sample, cache_k, cache_v, page_table, state_conv, state_dn, past_len, *weights)
    return (y_prompt, y_sample, k_prompt, v_prompt, dn_prompt, conv_prompt, k_sample, v_sample, dn_sample, conv_sample)
```

```python
import functools
import math

import jax
import jax.numpy as jnp
from jax import lax
from jax.experimental import pallas as pl
from jax.experimental.pallas import tpu as pltpu

F32 = jnp.float32
BF16 = jnp.bfloat16
I32 = jnp.int32
HIGHEST = lax.Precision.HIGHEST

HEAD_DIM = 128
ATT_GROUP = 2
MOBA_BLOCK = 256
MOBA_TOPK = 3
REL_BUCKETS = 32
REL_MAX_DIST = 2048
DN_CHUNK = 64
CONV_W = 4
N_EXPERTS = 16
EXPERTS_PER_GROUP = 4
LN_EPS = 1e-5
NORM_EPS = 1e-6
ROW_TILE = 256
MOE_TILE = 256
PAGES_PER_STEP = 16
NEG = -1e30
VMEM_LIMIT_V7X = 56 * 1024 * 1024


def _params(semantics):
    return pltpu.CompilerParams(dimension_semantics=semantics, vmem_limit_bytes=VMEM_LIMIT_V7X)


def _nt_dot(a, b, precision=None):
    return lax.dot_general(a, b, (((1,), (1,)), ((), ())), precision=precision, preferred_element_type=F32)


def _tn_dot(a, b, precision=None):
    return lax.dot_general(a, b, (((0,), (0,)), ((), ())), precision=precision, preferred_element_type=F32)


def _hdot(a, b):
    return jnp.dot(a, b, precision=HIGHEST, preferred_element_type=F32)


def _bdot(a, b):
    return jnp.dot(a.astype(BF16), b.astype(BF16), preferred_element_type=F32)


def _layer_norm(x, g, b):
    mu = jnp.mean(x, axis=-1, keepdims=True)
    xc = x - mu
    var = jnp.mean(xc * xc, axis=-1, keepdims=True)
    return xc * lax.rsqrt(var + LN_EPS) * g + b


def _silu(x):
    return x * (1.0 / (1.0 + jnp.exp(-x)))


def _sigmoid(x):
    return 1.0 / (1.0 + jnp.exp(-x))


def _softplus(x):
    return jnp.maximum(x, 0.0) + jnp.log(1.0 + jnp.exp(-jnp.abs(x)))


def _rel_bucket(dist):
    n = jnp.maximum(dist, 0)
    max_exact = REL_BUCKETS // 2
    nf = jnp.maximum(n, 1).astype(F32)
    large = max_exact + (jnp.log(nf / max_exact) / math.log(REL_MAX_DIST / max_exact)
                         * (REL_BUCKETS - max_exact)).astype(I32)
    large = jnp.minimum(large, REL_BUCKETS - 1)
    return jnp.where(n < max_exact, n, large)


def _ln_kernel(x_ref, g_ref, b_ref, o_ref):
    o_ref[...] = _layer_norm(x_ref[...], g_ref[...], b_ref[...])


def _input_layer_norm(x, g, b):
    nt, d = x.shape
    return pl.pallas_call(
        _ln_kernel,
        grid=(nt // ROW_TILE,),
        in_specs=[pl.BlockSpec((ROW_TILE, d), lambda i: (i, 0)),
                  pl.BlockSpec((1, d), lambda i: (0, 0)),
                  pl.BlockSpec((1, d), lambda i: (0, 0))],
        out_specs=pl.BlockSpec((ROW_TILE, d), lambda i: (i, 0)),
        out_shape=jax.ShapeDtypeStruct((nt, d), F32),
        compiler_params=_params(("parallel",)),
    )(x, g.reshape(1, d), b.reshape(1, d))


def _inproj_kernel(x_ref, w_ref, *o_refs):
    xb = x_ref[...].astype(BF16)
    off = 0
    for o_ref in o_refs:
        n = o_ref.shape[1]
        o_ref[...] = jnp.dot(xb, w_ref[:, off:off + n], preferred_element_type=F32)
        off += n


def _in_projection(x, w_bf16, widths):
    nt, d = x.shape
    cols = w_bf16.shape[1]
    assert sum(widths) == cols
    return pl.pallas_call(
        _inproj_kernel,
        grid=(nt // ROW_TILE,),
        in_specs=[pl.BlockSpec((ROW_TILE, d), lambda i: (i, 0)),
                  pl.BlockSpec((d, cols), lambda i: (0, 0), pipeline_mode=pl.Buffered(1))],
        out_specs=[pl.BlockSpec((ROW_TILE, n), lambda i: (i, 0)) for n in widths],
        out_shape=[jax.ShapeDtypeStruct((nt, n), F32) for n in widths],
        compiler_params=_params(("parallel",)),
    )(x, w_bf16)


def _bias_tile_kernel(tab_ref, o_ref):
    h = pl.program_id(0)
    d = pl.program_id(1)
    r = lax.broadcasted_iota(I32, (MOBA_BLOCK, MOBA_BLOCK), 0)
    c = lax.broadcasted_iota(I32, (MOBA_BLOCK, MOBA_BLOCK), 1)
    dist = d * MOBA_BLOCK + r - c
    bucket = _rel_bucket(dist)
    acc = jnp.zeros((MOBA_BLOCK, MOBA_BLOCK), F32)
    for k in range(REL_BUCKETS):
        acc = jnp.where(bucket == k, tab_ref[k, h], acc)
    o_ref[0, 0] = jnp.where(dist >= 0, acc, NEG)


def _bias_tiles(rel_bias, n_blk):
    heads = rel_bias.shape[1]
    return pl.pallas_call(
        _bias_tile_kernel,
        grid=(heads, n_blk),
        in_specs=[pl.BlockSpec(memory_space=pltpu.SMEM)],
        out_specs=pl.BlockSpec((1, 1, MOBA_BLOCK, MOBA_BLOCK), lambda h, d: (h, d, 0, 0)),
        out_shape=jax.ShapeDtypeStruct((heads, n_blk, MOBA_BLOCK, MOBA_BLOCK), F32),
        compiler_params=_params(("parallel", "parallel")),
    )(rel_bias.astype(F32))


def _moba_prompt_kernel(q_ref, k_ref, v_ref, bias_ref, o_ref, kaug_ref, vb_ref, kmean_ref):
    i = pl.program_id(2)
    seq = k_ref.shape[0]
    n_blk = seq // MOBA_BLOCK
    scale = HEAD_DIM ** -0.5

    @pl.when(i == 0)
    def _():
        kaug_ref[:, :HEAD_DIM] = k_ref[...].astype(BF16)
        rows = lax.broadcasted_iota(I32, (seq, HEAD_DIM), 0) // MOBA_BLOCK
        lanes = lax.broadcasted_iota(I32, (seq, HEAD_DIM), 1)
        kaug_ref[:, HEAD_DIM:] = jnp.where(rows == lanes, 1.0, 0.0).astype(BF16)
        vb_ref[...] = v_ref[...].astype(BF16)
        for j in range(n_blk):
            kmean_ref[j:j + 1, :] = jnp.mean(k_ref[j * MOBA_BLOCK:(j + 1) * MOBA_BLOCK, :], axis=0, keepdims=True)

    q = q_ref[...]
    gate = _nt_dot(kmean_ref[...].astype(BF16), q.astype(BF16))
    jidx = lax.broadcasted_iota(I32, gate.shape, 0)
    rank = jnp.zeros(gate.shape, I32)
    for jp in range(n_blk):
        row = gate[jp:jp + 1, :]
        before = (row > gate) | ((row == gate) & (jp < jidx))
        rank = rank + jnp.where(before, jnp.where(jp < i, 1, 0), 0)
    sel = ((jidx < i) & (rank < MOBA_TOPK)) | (jidx == i)
    selneg = jnp.where(sel, 0.0, NEG)
    selneg = jnp.concatenate([selneg, jnp.zeros((HEAD_DIM - n_blk, gate.shape[1]), F32)], axis=0)
    q_aug = jnp.concatenate([q.astype(BF16), selneg.T.astype(BF16)], axis=1)

    tq = q.shape[0]

    def body(t, carry):
        m, l, acc = carry
        start = pl.multiple_of((i - t) * MOBA_BLOCK, MOBA_BLOCK)
        s = _nt_dot(q_aug, kaug_ref[pl.ds(start, MOBA_BLOCK), :]) * scale + bias_ref[0, t]
        m_new = jnp.maximum(m, jnp.max(s, axis=1, keepdims=True))
        alpha = jnp.exp(m - m_new)
        p = jnp.exp(s - m_new)
        l = alpha * l + jnp.sum(p, axis=1, keepdims=True)
        acc = alpha * acc + jnp.dot(p.astype(BF16), vb_ref[pl.ds(start, MOBA_BLOCK), :],
                                    preferred_element_type=F32)
        return m_new, l, acc

    init = (jnp.full((tq, 1), -jnp.inf, F32), jnp.zeros((tq, 1), F32), jnp.zeros((tq, HEAD_DIM), F32))
    _, l, acc = lax.fori_loop(0, i + 1, body, init)
    o_ref[...] = (acc / l).astype(o_ref.dtype)


def _moba_prompt(qa, ka, va, bias_tiles, batch, seq):
    heads = qa.shape[1] // HEAD_DIM
    n_blk = seq // MOBA_BLOCK
    assert seq % MOBA_BLOCK == 0 and n_blk <= HEAD_DIM
    return pl.pallas_call(
        _moba_prompt_kernel,
        grid=(heads, batch, n_blk),
        in_specs=[pl.BlockSpec((MOBA_BLOCK, HEAD_DIM), lambda h, b, i: (b * n_blk + i, h)),
                  pl.BlockSpec((seq, HEAD_DIM), lambda h, b, i: (b, h // ATT_GROUP)),
                  pl.BlockSpec((seq, HEAD_DIM), lambda h, b, i: (b, h // ATT_GROUP)),
                  pl.BlockSpec((1, n_blk, MOBA_BLOCK, MOBA_BLOCK), lambda h, b, i: (h, 0, 0, 0))],
        out_specs=pl.BlockSpec((MOBA_BLOCK, HEAD_DIM), lambda h, b, i: (b * n_blk + i, h)),
        out_shape=jax.ShapeDtypeStruct((batch * seq, heads * HEAD_DIM), BF16),
        scratch_shapes=[pltpu.VMEM((seq, 2 * HEAD_DIM), BF16),
                        pltpu.VMEM((seq, HEAD_DIM), BF16),
                        pltpu.VMEM((n_blk, HEAD_DIM), F32)],
        compiler_params=_params(("parallel", "parallel", "arbitrary")),
    )(qa, ka, va, bias_tiles)


def _page_sum_kernel(pt_ref, *refs):
    o_ref = refs[-1]
    for u in range(0, PAGES_PER_STEP, 2):
        blk = (jnp.sum(refs[u][0], axis=0, keepdims=True) + jnp.sum(refs[u + 1][0], axis=0, keepdims=True))
        o_ref[0, u // 2:u // 2 + 1, :] = blk


def _past_block_sums(cache3, pt_abs):
    n_seq, n_pages = pt_abs.shape
    page, width = cache3.shape[1], cache3.shape[2]
    assert 2 * page == MOBA_BLOCK and n_pages % PAGES_PER_STEP == 0
    steps = n_pages // PAGES_PER_STEP

    def page_spec(u):
        return pl.BlockSpec((1, page, width), lambda s, t, pt: (pt[s, t * PAGES_PER_STEP + u], 0, 0))

    return pl.pallas_call(
        _page_sum_kernel,
        grid_spec=pltpu.PrefetchScalarGridSpec(
            num_scalar_prefetch=1,
            grid=(n_seq, steps),
            in_specs=[page_spec(u) for u in range(PAGES_PER_STEP)],
            out_specs=pl.BlockSpec((1, PAGES_PER_STEP // 2, width), lambda s, t, pt: (s, t, 0))),
        out_shape=jax.ShapeDtypeStruct((n_seq, n_pages // 2, width), F32),
        compiler_params=_params(("parallel", "arbitrary")),
    )(pt_abs, *([cache3] * PAGES_PER_STEP))


def _decode_select_kernel(q_ref, ksum_ref, o_ref):
    heads = q_ref.shape[1]
    n_blk = ksum_ref.shape[1]
    q = q_ref[0]
    qb = jnp.concatenate([q, q], axis=0).astype(BF16)
    hrow =lax.broadcasted_iota(I32, (heads, n_blk), 0)
    gate = jnp.zeros((heads, n_blk), F32)
    for g in range(heads // ATT_GROUP):
        kmean = ksum_ref[0, :, g * HEAD_DIM:(g + 1) * HEAD_DIM] * (1.0 / MOBA_BLOCK)
        gate = jnp.where(hrow // ATT_GROUP == g, _nt_dot(qb, kmean.astype(BF16))[:heads], gate)
    nidx = lax.broadcasted_iota(I32, (heads, n_blk), 1)
    rank = jnp.zeros((heads, n_blk), I32)
    for n in range(n_blk):
        col = gate[:, n:n + 1]
        rank = rank + jnp.where((col > gate) | ((col == gate) & (n < nidx)), 1, 0)
    lane = lax.broadcasted_iota(I32, (heads, HEAD_DIM), 1)
    out = jnp.zeros((heads, HEAD_DIM), I32)
    for r in range(MOBA_TOPK):
        idx = jnp.sum(jnp.where(rank == r, nidx.astype(F32), 0.0), axis=1, keepdims=True)
        out = jnp.where(lane == r, idx.astype(I32), out)
    o_ref[0] = out


def _decode_select(q3, ksum):
    n_seq, heads, _ = q3.shape
    n_blk, width = ksum.shape[1], ksum.shape[2]
    return pl.pallas_call(
        _decode_select_kernel,
        grid=(n_seq,),
        in_specs=[pl.BlockSpec((1, heads, HEAD_DIM), lambda s: (s, 0, 0)),
                  pl.BlockSpec((1, n_blk, width), lambda s: (s, 0, 0))],
        out_specs=pl.BlockSpec((1, heads, HEAD_DIM), lambda s: (s, 0, 0)),
        out_shape=jax.ShapeDtypeStruct((n_seq, heads, HEAD_DIM), I32),
        compiler_params=_params(("parallel",)),
    )(q3, ksum)


def _decode_attend_kernel(pages_ref, blk_ref, q_ref, kn_ref, vn_ref, tab_ref, *refs, past_len):
    n_tiles = 2 * MOBA_TOPK
    k_refs, v_refs, o_ref = refs[:n_tiles], refs[n_tiles:2 * n_tiles], refs[-1]
    s = pl.program_id(0)
    h = pl.program_id(1)
    page = k_refs[0].shape[1]
    scale = HEAD_DIM ** -0.5
    rows = 16
    bround = lambda a: a.astype(BF16).astype(F32)
    q = jnp.broadcast_to(q_ref[0, pl.ds(h, 1), :], (rows, HEAD_DIM)).astype(BF16)
    lane = lax.broadcasted_iota(I32, (1, page), 1)

    logits = []
    for u in range(n_tiles):
        kpos = blk_ref[s * pl.num_programs(1) + h, u // 2] * MOBA_BLOCK + (u % 2) * page + lane
        bucket = _rel_bucket(past_len - kpos)
        bias = jnp.zeros((1, page), F32)
        for k in range(REL_BUCKETS):
            bias = jnp.where(bucket == k, tab_ref[k, h], bias)
        logits.append(_nt_dot(q, k_refs[u][0].astype(BF16))[0:1, :] * scale + bias)
    kn = kn_ref[0, pl.ds(h // ATT_GROUP, 1), :]
    vn = vn_ref[0, pl.ds(h // ATT_GROUP, 1), :]
    own = jnp.sum(q[0:1, :].astype(F32) * bround(kn), axis=1, keepdims=True) * scale + tab_ref[0, h]

    m = own
    for lg in logits:
        m = jnp.maximum(m, jnp.max(lg, axis=1, keepdims=True))
    e_own = jnp.exp(own - m)
    es = [jnp.exp(lg - m) for lg in logits]
    den = e_own
    for e in es:
        den = den + jnp.sum(e, axis=1, keepdims=True)
    acc = bround(e_own / den) * bround(vn)
    for u in range(n_tiles):
        p = jnp.broadcast_to(es[u] / den, (rows, page)).astype(BF16)
        acc = acc + jnp.dot(p, v_refs[u][0].astype(BF16), preferred_element_type=F32)[0:1, :]
    o_ref[0, pl.ds(h, 1), :] = acc


def _decode_attend(q3, kn3, vn3, cache_k3, cache_v3, pages, blocks, rel_bias, past_len):
    n_seq, heads, _ = q3.shape
    kvh = kn3.shape[1]
    page = cache_k3.shape[1]
    n_tiles = 2 * MOBA_TOPK

    def tile_spec(u):
        return pl.BlockSpec((1, page, HEAD_DIM), lambda s, h, pg, bk: (pg[s * heads + h, u], 0, h // ATT_GROUP))

    return pl.pallas_call(
        functools.partial(_decode_attend_kernel, past_len=past_len),
        grid_spec=pltpu.PrefetchScalarGridSpec(
            num_scalar_prefetch=2,
            grid=(n_seq, heads),
            in_specs=[pl.BlockSpec((1, heads, HEAD_DIM), lambda s, h, pg, bk: (s, 0, 0)),
                      pl.BlockSpec((1, kvh, HEAD_DIM), lambda s, h, pg, bk: (s, 0, 0)),
                      pl.BlockSpec((1, kvh, HEAD_DIM), lambda s, h, pg, bk: (s, 0, 0)),
                      pl.BlockSpec(memory_space=pltpu.SMEM)]
                     + [tile_spec(u) for u in range(n_tiles)] * 2,
            out_specs=pl.BlockSpec((1, heads, HEAD_DIM), lambda s, h, pg, bk: (s, 0, 0))),
        out_shape=jax.ShapeDtypeStruct((n_seq, heads, HEAD_DIM), F32),
        compiler_params=_params(("parallel", "arbitrary")),
    )(pages.reshape(n_seq * heads, n_tiles), blocks.reshape(n_seq * heads, MOBA_TOPK), q3, kn3, vn3, rel_bias.astype(F32), *([cache_k3] * n_tiles), *([cache_v3] * n_tiles))


def _conv_gate_math(bg, al, dtb):
    lane = lax.broadcasted_iota(I32, bg.shape, 1)
    return jnp.where(lane < 8, _sigmoid(bg), -jnp.exp(al) * _softplus(bg + dtb))


def _head_l2norm(x, n_heads, mult):
    outs = []
    for h in range(n_heads):
        seg = x[:, h * HEAD_DIM:(h + 1) * HEAD_DIM]
        ss = jnp.sum(seg * seg, axis=1, keepdims=True)
        outs.append(seg * (lax.rsqrt(ss + NORM_EPS) * mult))
    return jnp.concatenate(outs, axis=1)


def _dn_prep_kernel(cur_ref, prev_ref, bg_ref, cw_ref, al_ref, dtb_ref, qn_ref, kn_ref, vv_ref, gb_ref):
    i = pl.program_id(1)
    t, width3 = cur_ref.shape
    width = width3 // 3
    n_heads = width // HEAD_DIM
    cur = cur_ref[...]
    prev = prev_ref[...] * jnp.where(i > 0, 1.0, 0.0)
    row8 = lax.broadcasted_iota(I32, (8, 1), 0)
    y = cur * cw_ref[CONV_W - 1:CONV_W, :]
    for k in range(1, CONV_W):
        rolled = pltpu.roll(cur, k, 0)
        head = jnp.where(row8 < k, pltpu.roll(prev, k, 0), rolled[:8])
        shifted = jnp.concatenate([head, rolled[8:]], axis=0)
        y = y + shifted * cw_ref[CONV_W - 1 - k:CONV_W - k, :]
    a = _silu(y)
    qn_ref[...] = _head_l2norm(a[:, :width], n_heads, HEAD_DIM ** -0.5)
    kn_ref[...] = _head_l2norm(a[:, width:2 * width], n_heads, 1.0)
    vv_ref[...] = a[:, 2 * width:]
    gb_ref[...] = _conv_gate_math(bg_ref[...], al_ref[...], dtb_ref[...])


def _dn_prep(qkvb, bg, conv_w, al_row, dtb_row, batch, seq):
    width3 = qkvb.shape[1]
    width = width3 // 3
    tiles = seq // ROW_TILE
    rows = batch * seq
    row_spec = lambda n: pl.BlockSpec((ROW_TILE, n), lambda b, i: (b * tiles + i, 0))
    return pl.pallas_call(
        _dn_prep_kernel,
        grid=(batch, tiles),
        in_specs=[row_spec(width3),
                  pl.BlockSpec((8, width3), lambda b, i: (jnp.maximum((b * tiles + i) * (ROW_TILE // 8) - 1, 0), 0)),
                  row_spec(HEAD_DIM),
                  pl.BlockSpec((CONV_W, width3), lambda b, i: (0, 0)),
                  pl.BlockSpec((1, HEAD_DIM), lambda b, i: (0, 0)),
                  pl.BlockSpec((1, HEAD_DIM), lambda b, i: (0, 0))],
        out_specs=[row_spec(width), row_spec(width), row_spec(width), row_spec(HEAD_DIM)],
        out_shape=[jax.ShapeDtypeStruct((rows, width), F32)] * 3 + [jax.ShapeDtypeStruct((rows, HEAD_DIM), F32)],
        compiler_params=_params(("parallel", "parallel")),
    )(qkvb, qkvb, bg, conv_w, al_row, dtb_row)


def _dn_chunk_kernel(qn_ref, kn_ref, vv_ref, gb_ref, og_ref, nw_ref, y_ref, s_out_ref, s_ref):
    c = pl.program_id(1)
    n_chunks = pl.num_programs(1)
    ch, width = qn_ref.shape
    n_heads = width // HEAD_DIM

    @pl.when(c == 0)
    def _():
        s_ref[...] = jnp.zeros(s_ref.shape, F32)

    ri = lax.broadcasted_iota(I32, (ch, ch), 0)
    ci = lax.broadcasted_iota(I32, (ch, ch), 1)
    tri = ri >= ci
    strict = ri > ci
    eye = ri == ci
    gbv = gb_ref[...]
    gcum_all = _hdot(jnp.where(tri, 1.0, 0.0), gbv)

    for h in range(n_heads):
        sl = slice(h * HEAD_DIM, (h + 1) * HEAD_DIM)
        q, k, v = qn_ref[:, sl], kn_ref[:, sl], vv_ref[:, sl]
        beta = gbv[:, h:h + 1]
        gc = gcum_all[:, 8 + h:9 + h]
        gcb = jnp.broadcast_to(gc, (ch, ch))
        grow = jnp.sum(jnp.where(eye, gcb, 0.0), axis=0, keepdims=True)
        decay = jnp.where(tri, jnp.exp(jnp.where(tri, gcb - grow, 0.0)), 0.0)
        kbeta = k * beta
        egc = jnp.exp(gc)
        kb16 = k.astype(BF16)
        a = jnp.where(strict, _nt_dot(kbeta.astype(BF16), kb16) * decay, 0.0)
        x = jnp.concatenate([v * beta, kbeta * egc], axis=1)
        npow = -a
        span = 1
        while True:
            x = x + _hdot(npow, x)
            span *= 2
            if span >= ch:
                break
            npow = _hdot(npow, npow)
        u, w = x[:, :HEAD_DIM], x[:, HEAD_DIM:]
        s = s_ref[h]
        s16 = s.astype(BF16)
        v_new = u - jnp.dot(w.astype(BF16), s16, preferred_element_type=F32)
        v16 = v_new.astype(BF16)
        attn = jnp.where(tri, _nt_dot(q.astype(BF16), kb16) * decay, 0.0)
        o = (jnp.dot((q * egc).astype(BF16), s16, preferred_element_type=F32)
             + jnp.dot(attn.astype(BF16), v16, preferred_element_type=F32))
        g_last = gc[ch - 1:ch, :]
        s_ref[h] = s * jnp.exp(g_last) + _tn_dot((k * jnp.exp(g_last - gc)).astype(BF16), v16)
        o = o * lax.rsqrt(jnp.mean(o * o, axis=1, keepdims=True) + NORM_EPS) * nw_ref[...]
        y_ref[:, sl] = (o * _silu(og_ref[:, sl])).astype(y_ref.dtype)

    @pl.when(c == n_chunks - 1)
    def _():
        s_out_ref[0] = s_ref[...]


def _dn_chunks(qn, kn, vv, gb, og, norm_w, batch, seq):
    width = qn.shape[1]
    n_heads = width // HEAD_DIM
    assert seq % DN_CHUNK == 0
    n_chunks = seq // DN_CHUNK
    row_spec = lambda n: pl.BlockSpec((DN_CHUNK, n), lambda b, c: (b * n_chunks + c, 0))
    return pl.pallas_call(
        _dn_chunk_kernel,
        grid=(batch, n_chunks),
        in_specs=[row_spec(width), row_spec(width), row_spec(width), row_spec(HEAD_DIM), row_spec(width),
                  pl.BlockSpec((1, HEAD_DIM), lambda b, c: (0, 0))],
        out_specs=[row_spec(width),
                   pl.BlockSpec((1, n_heads, HEAD_DIM, HEAD_DIM), lambda b, c: (b, 0, 0, 0))],
        out_shape=[jax.ShapeDtypeStruct((batch * seq, width), BF16),
                   jax.ShapeDtypeStruct((batch, n_heads, HEAD_DIM, HEAD_DIM), F32)],
        scratch_shapes=[pltpu.VMEM((n_heads, HEAD_DIM, HEAD_DIM), F32)],
        compiler_params=_params(("parallel", "arbitrary")),
    )(qn, kn, vv, gb, og, norm_w)


def _dn_step_kernel(x_ref, cs_ref, bg_ref, og_ref, s_ref, cw_ref, al_ref, dtb_ref, nw_ref,
                    y_ref, s_out_ref, cs_out_ref):
    width3 = x_ref.shape[2]
    width = width3 // 3
    n_heads = width // HEAD_DIM
    xn = x_ref[0]
    cs = cs_ref[0]
    y = xn * cw_ref[CONV_W - 1:CONV_W, :]
    for j in range(CONV_W - 1):
        y = y + cs[j:j + 1, :] * cw_ref[j:j + 1, :]
    for j in range(1, CONV_W - 1):
        cs_out_ref[0, j - 1:j, :] = cs_ref[0, j:j + 1, :]
    cs_out_ref[0, CONV_W - 2:CONV_W - 1, :] = xn
    a = _silu(y)
    qn = _head_l2norm(a[:, :width], n_heads, HEAD_DIM ** -0.5)
    kn = _head_l2norm(a[:, width:2 * width], n_heads, 1.0)
    vv = a[:, 2 * width:]
    gbv = _conv_gate_math(bg_ref[0], al_ref[...], dtb_ref[...])
    og = og_ref[0]
    rows = 16
    rowi = lax.broadcasted_iota(I32, (rows, HEAD_DIM), 0)
    bround = lambda t: t.astype(BF16).astype(F32)
    for h in range(n_heads):
        sl = slice(h * HEAD_DIM, (h + 1) * HEAD_DIM)
        q, k, v = qn[:, sl], kn[:, sl], vv[:, sl]
        beta = gbv[:, h:h + 1]
        eg = jnp.exp(gbv[:, 8 + h:9 + h])
        s = s_ref[0, h]
        lhs = jnp.where(rowi == 0, k * beta * eg, jnp.where(rowi == 1, q * eg, 0.0))
        prod = jnp.dot(lhs.astype(BF16), s.astype(BF16), preferred_element_type=F32)
        v_new = v * beta - prod[0:1, :]
        attn = jnp.sum(bround(q) * bround(k), axis=1, keepdims=True)
        o = prod[1:2, :] + bround(attn) * bround(v_new)
        outer = _tn_dot(jnp.where(rowi == 0, k, 0.0).astype(BF16),
                        jnp.broadcast_to(v_new, (rows, HEAD_DIM)).astype(BF16))
        s_out_ref[0, h] = s * eg + outer
        o = o * lax.rsqrt(jnp.mean(o * o, axis=1, keepdims=True) + NORM_EPS) * nw_ref[...]
        y_ref[0, :, sl] = o * _silu(og[:, sl])


def _dn_step(x3, conv_state, bg3, og3, dn_state, conv_w, al_row, dtb_row, norm_w):
    n_seq, _, width3 = x3.shape
    width = width3 // 3
    n_heads = width // HEAD_DIM
    vec = lambda n: pl.BlockSpec((1, 1, n), lambda s: (s, 0, 0))
    const = lambda r, n: pl.BlockSpec((r, n), lambda s: (0, 0))
    return pl.pallas_call(
        _dn_step_kernel,
        grid=(n_seq,),
        in_specs=[vec(width3),
                  pl.BlockSpec((1, CONV_W - 1, width3), lambda s: (s, 0, 0)),
                  vec(HEAD_DIM), vec(width),
                  pl.BlockSpec((1, n_heads, HEAD_DIM, HEAD_DIM), lambda s: (s, 0, 0, 0)),
                  const(CONV_W, width3), const(1, HEAD_DIM), const(1, HEAD_DIM), const(1, HEAD_DIM)],
        out_specs=[vec(width),
                   pl.BlockSpec((1, n_heads, HEAD_DIM, HEAD_DIM), lambda s: (s, 0, 0, 0)),
                   pl.BlockSpec((1, CONV_W - 1, width3), lambda s: (s, 0, 0))],
        out_shape=[jax.ShapeDtypeStruct((n_seq, 1, width), F32),
                   jax.ShapeDtypeStruct(dn_state.shape, F32),
                   jax.ShapeDtypeStruct(conv_state.shape, F32)],
        compiler_params=_params(("parallel",)),
    )(x3, conv_state, bg3, og3, dn_state, conv_w, al_row, dtb_row, norm_w)


def _top2_sum(a, b, c, d):
    hi1, lo1 = jnp.maximum(a, b), jnp.minimum(a, b)
    hi2, lo2 = jnp.maximum(c, d), jnp.minimum(c, d)
    return jnp.maximum(hi1, hi2) + jnp.maximum(jnp.minimum(hi1, hi2), jnp.maximum(lo1, lo2))


def _outproj_router_kernel(y_ref, x_ref, w_ref, g_ref, b_ref, wr_ref, rb_ref,
                           x1_ref, info_ref, infot_ref, cnt_ref, carry_ref, *, alpha, n_valid):
    i = pl.program_id(0)
    tm = x_ref.shape[0]

    @pl.when(i == 0)
    def _():
        carry_ref[...] = jnp.zeros(carry_ref.shape, F32)

    mixed = jnp.dot(y_ref[...], w_ref[...], preferred_element_type=F32)
    x1 = _layer_norm(alpha * x_ref[...] + mixed, g_ref[...], b_ref[...])
    x1_ref[...] = x1

    scores = _sigmoid(_nt_dot(wr_ref[...], x1.astype(BF16)))
    selv = scores + rb_ref[:, 0:1]
    sel = [selv[e:e + 1, :] for e in range(N_EXPERTS)]
    sc = [scores[e:e + 1, :] for e in range(N_EXPERTS)]
    n_groups = N_EXPERTS // EXPERTS_PER_GROUP
    gs = [_top2_sum(*sel[EXPERTS_PER_GROUP * g:EXPERTS_PER_GROUP * (g + 1)]) for g in range(n_groups)]
    best, bidx = gs[0], jnp.zeros((1, tm), I32)
    for g in range(1, n_groups):
        better = gs[g] > best
        best = jnp.where(better, gs[g], best)
        bidx = jnp.where(better, g, bidx)
    ninf = jnp.full((1, tm), -jnp.inf, F32)
    vals = [jnp.where(bidx == e // EXPERTS_PER_GROUP, sel[e], ninf) for e in range(N_EXPERTS)]
    b1, i1, s1 = vals[0], jnp.zeros((1, tm), I32), sc[0]
    for e in range(1, N_EXPERTS):
        better = vals[e] > b1
        b1 = jnp.where(better, vals[e], b1)
        i1 = jnp.where(better, e, i1)
        s1 = jnp.where(better, sc[e], s1)
    b2, i2, s2 = ninf, jnp.zeros((1, tm), I32), jnp.zeros((1, tm), F32)
    for e in range(N_EXPERTS):
        cand = jnp.where(i1 == e, ninf, vals[e])
        better = cand > b2
        b2 = jnp.where(better, cand, b2)
        i2 = jnp.where(better, e, i2)
        s2 = jnp.where(better, sc[e], s2)
    tok = i * tm + lax.broadcasted_iota(I32, (1, tm), 1)
    valid = tok < n_valid
    inv = 1.0 / (s1 + s2)
    g1 = jnp.where(valid, s1 * inv, 0.0)
    g2 = jnp.where(valid, s2 * inv, 0.0)

    erow = lax.broadcasted_iota(I32, (N_EXPERTS, tm), 0)
    oh1 = jnp.where((erow == i1) & valid, 1.0, 0.0)
    oh2 = jnp.where((erow == i2) & valid, 1.0, 0.0)
    both = oh1 + oh2
    upper = jnp.where(lax.broadcasted_iota(I32, (tm, tm), 0) <= lax.broadcasted_iota(I32, (tm, tm), 1), 1.0, 0.0)
    incl = jnp.dot(both.astype(BF16), upper.astype(BF16), preferred_element_type=F32)
    before = carry_ref[:, 0:1] + incl - both
    r1 = jnp.sum(oh1 * before, axis=0, keepdims=True)
    r2 = jnp.sum(oh2 * before, axis=0, keepdims=True)
    carry_ref[...] = carry_ref[...] + incl[:, tm - 1:tm]
    cnt_ref[...] = carry_ref[...]

    row8 = lax.broadcasted_iota(I32, (8, tm), 0)
    info = jnp.zeros((8, tm), F32)
    for r, val in enumerate((i1.astype(F32), i2.astype(F32), g1, g2, r1, r2)):
        info = jnp.where(row8 == r, val, info)
    info_ref[...] = info
    infot_ref[...] = jnp.concatenate([info, jnp.zeros((HEAD_DIM - 8, tm), F32)], axis=0).T


def _outproj_router(y, x, w_out_bf16, ln_g, ln_b, wr_t, rb_col, alpha, n_valid):
    nt, d = x.shape
    dm = y.shape[1]
    kern = functools.partial(_outproj_router_kernel, alpha=alpha, n_valid=n_valid)
    return pl.pallas_call(
        kern,
        grid=(nt // ROW_TILE,),
        in_specs=[pl.BlockSpec((ROW_TILE, dm), lambda i: (i, 0)),
                  pl.BlockSpec((ROW_TILE, d), lambda i: (i, 0)),
                  pl.BlockSpec((dm, d), lambda i: (0, 0)),
                  pl.BlockSpec((1, d), lambda i: (0, 0)),
                  pl.BlockSpec((1, d), lambda i: (0, 0)),
                  pl.BlockSpec((N_EXPERTS, d), lambda i: (0, 0)),
                  pl.BlockSpec((N_EXPERTS, HEAD_DIM), lambda i: (0, 0))],
        out_specs=[pl.BlockSpec((ROW_TILE, d), lambda i: (i, 0)),
                   pl.BlockSpec((8, ROW_TILE), lambda i: (0, i)),
                   pl.BlockSpec((ROW_TILE, HEAD_DIM), lambda i: (i, 0)),
                   pl.BlockSpec((N_EXPERTS, HEAD_DIM), lambda i: (0, 0))],
        out_shape=[jax.ShapeDtypeStruct((nt, d), F32),
                   jax.ShapeDtypeStruct((8, nt), F32),
                   jax.ShapeDtypeStruct((nt, HEAD_DIM), F32),
                   jax.ShapeDtypeStruct((N_EXPERTS, HEAD_DIM), F32)],
        scratch_shapes=[pltpu.VMEM((N_EXPERTS, HEAD_DIM), F32)],
        compiler_params=_params(("arbitrary",)),
    )(y, x, w_out_bf16, ln_g, ln_b, wr_t, rb_col)


def _row_copy(src_hbm, src_row, dst_ref, dst_row, sem):
    return pltpu.make_async_copy(src_hbm.at[pl.ds(src_row, 1), :], dst_ref.at[pl.ds(dst_row, 1), :], sem)


def _moe_gather_kernel(rt_ref, nu_ref, x_hbm, o_ref, sem):
    m = pl.program_id(0)
    tm = o_ref.shape[0]

    @pl.when(m < nu_ref[0])
    def _():
        def issue(r, carry):
            _row_copy(x_hbm, rt_ref[m * tm + r], o_ref, r, sem).start()
            return carry

        lax.fori_loop(0, tm, issue, 0)

        def drain(r, carry):
            _row_copy(x_hbm, 0, o_ref, r, sem).wait()
            return carry

        lax.fori_loop(0, tm, drain, 0)

    @pl.when(m >= nu_ref[0])
    def _():
        o_ref[...] = jnp.zeros(o_ref.shape, o_ref.dtype)


def _moe_gather(row_tok, n_used, x):
    d = x.shape[1]
    n_rows = row_tok.shape[0]
    return pl.pallas_call(
        _moe_gather_kernel,
        grid_spec=pltpu.PrefetchScalarGridSpec(
            num_scalar_prefetch=2,
            grid=(n_rows // MOE_TILE,),
            in_specs=[pl.BlockSpec(memory_space=pl.ANY)],
            out_specs=pl.BlockSpec((MOE_TILE, d), lambda m, rt, nu: (m, 0)),
            scratch_shapes=[pltpu.SemaphoreType.DMA(())]),
        out_shape=jax.ShapeDtypeStruct((n_rows, d), x.dtype),
        compiler_params=_params(("arbitrary",)),
    )(row_tok, n_used, x)


def _moe_expert_kernel(be_ref, nu_ref, x_ref, wg_ref, wu_ref, wd_ref, o_ref):
    m = pl.program_id(0)

    @pl.when(m < nu_ref[0])
    def _():
        xb = x_ref[...].astype(BF16)
        hid = _silu(jnp.dot(xb, wg_ref[0], preferred_element_type=F32)) * jnp.dot(
            xb, wu_ref[0], preferred_element_type=F32)
        o_ref[...] = jnp.dot(hid.astype(BF16), wd_ref[0], preferred_element_type=F32)

    @pl.when(m >= nu_ref[0])
    def _():
        o_ref[...] = jnp.zeros(o_ref.shape, o_ref.dtype)


def _moe_experts(blk_e, n_used, xs, wg, wu, wd):
    n_rows, d = xs.shape
    f = wg.shape[2]
    last = lambda m, nu: jnp.minimum(m, nu[0] - 1)
    return pl.pallas_call(
        _moe_expert_kernel,
        grid_spec=pltpu.PrefetchScalarGridSpec(
            num_scalar_prefetch=2,
            grid=(n_rows // MOE_TILE,),
            in_specs=[pl.BlockSpec((MOE_TILE, d), lambda m, be, nu: (last(m, nu), 0)),
                      pl.BlockSpec((1, d, f), lambda m, be, nu: (be[m], 0, 0)),
                      pl.BlockSpec((1, d, f), lambda m, be, nu: (be[m], 0, 0)),
                      pl.BlockSpec((1, f, d), lambda m, be, nu: (be[m], 0, 0))],
            out_specs=pl.BlockSpec((MOE_TILE, d), lambda m, be, nu: (m, 0))),
        out_shape=jax.ShapeDtypeStruct((n_rows, d), F32),
        compiler_params=_params(("arbitrary",)),
    )(blk_e, n_used, xs, wg, wu, wd)


def _moe_combine_kernel(dest_ref, y_hbm, x_ref, it_ref, g_ref, b_ref, o_ref, buf_ref, sem, *, alpha, nt):
    i = pl.program_id(0)
    tm = x_ref.shape[0]

    def issue(r, carry):
        _row_copy(y_hbm, dest_ref[i * tm + r], buf_ref.at[0], r, sem).start()
        _row_copy(y_hbm, dest_ref[nt + i * tm + r], buf_ref.at[1], r, sem).start()
        return carry

    lax.fori_loop(0, tm, issue, 0)

    def drain(r, carry):
        _row_copy(y_hbm, 0, buf_ref.at[0], r, sem).wait()
        _row_copy(y_hbm, 0, buf_ref.at[1], r, sem).wait()
        return carry

    lax.fori_loop(0, tm, drain, 0)
    it = it_ref[...]
    f = buf_ref[0] * it[:, 2:3] + buf_ref[1] * it[:, 3:4]
    o_ref[...] = _layer_norm(alpha * x_ref[...] + f, g_ref[...], b_ref[...])


def _moe_combine(dest, y_sorted, x1, info_t, ln_g, ln_b, alpha):
    nt, d = x1.shape
    kern = functools.partial(_moe_combine_kernel, alpha=alpha, nt=nt)
    return pl.pallas_call(
        kern,
        grid_spec=pltpu.PrefetchScalarGridSpec(
            num_scalar_prefetch=1,
            grid=(nt // ROW_TILE,),
            in_specs=[pl.BlockSpec(memory_space=pl.ANY),
                      pl.BlockSpec((ROW_TILE, d), lambda i, ds: (i, 0)),
                      pl.BlockSpec((ROW_TILE, HEAD_DIM), lambda i, ds: (i, 0)),
                      pl.BlockSpec((1, d), lambda i, ds: (0, 0)),
                      pl.BlockSpec((1, d), lambda i, ds: (0, 0))],
            out_specs=pl.BlockSpec((ROW_TILE, d), lambda i, ds: (i, 0)),
            scratch_shapes=[pltpu.VMEM((2, ROW_TILE, d), F32), pltpu.SemaphoreType.DMA(())]),
        out_shape=jax.ShapeDtypeStruct((nt, d), F32),
        compiler_params=_params(("arbitrary",)),
    )(dest, y_sorted, x1, info_t, ln_g, ln_b)


def _moe_layer(y_mix, x, w_out_bf16, ln1_g, ln1_b, wr_t, rb_col, wg, wu, wd, ln2_g, ln2_b, alpha, n_valid):
    nt, d = x.shape
    x1, info, info_t, cnt = _outproj_router(y_mix, x, w_out_bf16, ln1_g, ln1_b, wr_t, rb_col, alpha, n_valid)
    n_blocks = -(-(2 * n_valid) // MOE_TILE) + N_EXPERTS
    counts = cnt[:, 0].astype(I32)
    padded = (counts + MOE_TILE - 1) // MOE_TILE * MOE_TILE
    pad_end = jnp.cumsum(padded)
    pad_start = pad_end - padded
    e1, e2 = info[0].astype(I32), info[1].astype(I32)
    dest1 = pad_start[e1] + info[4].astype(I32)
    dest2 = pad_start[e2] + info[5].astype(I32)
    tok = jnp.arange(n_valid, dtype=I32)
    row_tok = jnp.zeros((n_blocks * MOE_TILE,), I32).at[dest1[:n_valid]].set(tok).at[dest2[:n_valid]].set(tok)
    n_used = (pad_end[-1] // MOE_TILE).astype(I32)
    blk_start = jnp.arange(n_blocks, dtype=I32) * MOE_TILE
    blk_e = jnp.minimum(jnp.sum(pad_end[None, :] <= blk_start[:, None], axis=1), N_EXPERTS - 1).astype(I32)
    blk_e = jnp.where(jnp.arange(n_blocks) < n_used, blk_e, blk_e[jnp.maximum(n_used - 1, 0)])
    valid = jnp.arange(nt) < n_valid
    dest = jnp.concatenate([jnp.where(valid, dest1, 0), jnp.where(valid, dest2, 0)]).astype(I32)
    nu = n_used.reshape(1)
    xs = _moe_gather(row_tok, nu, x1)
    ys = _moe_experts(blk_e, nu, xs, wg, wu, wd)
    return _moe_combine(dest, ys, x1, info_t, ln2_g, ln2_b, alpha)


def kernel(x_prompt, x_sample, cache_k, cache_v, state_dn, state_conv, page_table, ln_in_g, ln_in_b, rel_bias,
           w_router, router_bias, w_in, w_out, conv_w, a_log, dt_bias, dn_norm_w, ln1_g, ln1_b, ln2_g, ln2_b,
           w_gate, w_up, w_down):
    batch, seq, d = x_prompt.shape
    n_seq, dec_seq, _ = x_sample.shape
    depth = w_in.shape[0]
    n_pool, page = cache_k.shape[1], cache_k.shape[2]
    kvh = cache_k.shape[3]
    n_pages = page_table.shape[1]
    past_len = n_pages * page
    assert dec_seq == 1 and past_len % MOBA_BLOCK == 0 and n_seq <= ROW_TILE
    att_w = d // 2
    dn_w = d - att_w
    heads = att_w // HEAD_DIM
    dn_heads = dn_w // HEAD_DIM
    kv_w = kvh * HEAD_DIM
    assert dn_heads == 8 and heads == ATT_GROUP * kvh
    alpha = (2 * depth) ** 0.25
    n_prompt = batch * seq
    nt = n_prompt + ROW_TILE
    n_valid = n_prompt + n_seq

    x = jnp.concatenate([x_prompt.reshape(n_prompt, d), x_sample.reshape(n_seq, d),
                         jnp.zeros((ROW_TILE - n_seq, d), F32)], axis=0)
    x = _input_layer_norm(x, ln_in_g, ln_in_b)

    widths = (att_w, kv_w, kv_w, 3 * dn_w, dn_w, HEAD_DIM)
    in_cols = w_in.shape[2]
    w_in_b = jnp.pad(w_in, ((0, 0), (0, 0), (0, sum(widths) - in_cols))).astype(BF16)
    w_out_b = w_out.astype(BF16)
    wg_b, wu_b, wd_b = w_gate.astype(BF16), w_up.astype(BF16), w_down.astype(BF16)
    wr_t = w_router.T.astype(BF16)
    rb_col = jnp.broadcast_to(router_bias.astype(F32)[:, None], (N_EXPERTS, HEAD_DIM))
    lane_pad = lambda v: jnp.pad(v.astype(F32), ((0, 0), (dn_heads, HEAD_DIM - 2 * dn_heads)))
    al_rows, dtb_rows = lane_pad(a_log), lane_pad(dt_bias)
    bias_tiles = _bias_tiles(rel_bias, seq // MOBA_BLOCK)
    cache_k3 = cache_k.reshape(depth * n_pool, page, kv_w)
    cache_v3 = cache_v.reshape(depth * n_pool, page, kv_w)

    ks_p, vs_p, dn_p, cv_p, ks_s, vs_s, dn_s, cv_s = [], [], [], [], [], [], [], []
    for l in range(depth):
        qa, ka, va, qkvb, og, bg = _in_projection(x, w_in_b[l], widths)
        al_row, dtb_row = al_rows[l:l + 1], dtb_rows[l:l + 1]
        norm_w = dn_norm_w[l].reshape(1, HEAD_DIM).astype(F32)

        ya_p = _moba_prompt(qa, ka, va, bias_tiles, batch, seq)
        qn, kn, vv, gb = _dn_prep(qkvb, bg, conv_w[l], al_row, dtb_row, batch, seq)
        yb_p, s_p = _dn_chunks(qn, kn, vv, gb, og, norm_w, batch, seq)

        sl = slice(n_prompt, n_prompt + n_seq)
        q3 = qa[sl].reshape(n_seq, heads, HEAD_DIM)
        kn3 = ka[sl].reshape(n_seq, kvh, HEAD_DIM)
        vn3 = va[sl].reshape(n_seq, kvh, HEAD_DIM)
        pt_abs = page_table.astype(I32) + l * n_pool
        ksum = _past_block_sums(cache_k3, pt_abs)
        top = _decode_select(q3, ksum)[:, :, :MOBA_TOPK]
        pg_idx = jnp.stack([2 * top, 2 * top + 1], axis=-1).reshape(n_seq, heads, 2 * MOBA_TOPK)
        pages = jnp.take_along_axis(pt_abs[:, None, :], pg_idx, axis=2).astype(I32)
        ya_s = _decode_attend(q3, kn3, vn3, cache_k3, cache_v3, pages, top, rel_bias, past_len)
        yb_s, s_s, c_s = _dn_step(qkvb[sl].reshape(n_seq, 1, 3 * dn_w), state_conv[l],
                                  bg[sl].reshape(n_seq, 1, HEAD_DIM), og[sl].reshape(n_seq, 1, dn_w),
                                  state_dn[l], conv_w[l], al_row, dtb_row, norm_w)

        y_s = jnp.concatenate([ya_s.reshape(n_seq, att_w), yb_s.reshape(n_seq, dn_w)], axis=1).astype(BF16)
        y_mix = jnp.concatenate([jnp.concatenate([ya_p, yb_p], axis=1), y_s,
                                 jnp.zeros((ROW_TILE - n_seq, d), BF16)], axis=0)
        x = _moe_layer(y_mix, x, w_out_b[l], ln1_g[l].reshape(1, d), ln1_b[l].reshape(1, d), wr_t, rb_col,
                       wg_b[l], wu_b[l], wd_b[l], ln2_g[l].reshape(1, d), ln2_b[l].reshape(1, d), alpha, n_valid)

        ks_p.append(ka[:n_prompt].reshape(batch, seq, kvh, HEAD_DIM))
        vs_p.append(va[:n_prompt].reshape(batch, seq, kvh, HEAD_DIM))
        dn_p.append(s_p)
        cv_p.append(qkvb[:n_prompt].reshape(batch, seq, 3 * dn_w)[:, seq - (CONV_W - 1):])
        ks_s.append(kn3.reshape(n_seq, 1, kvh, HEAD_DIM))
        vs_s.append(vn3.reshape(n_seq, 1, kvh, HEAD_DIM))
        dn_s.append(s_s)
        cv_s.append(c_s)

    y_prompt = x[:n_prompt].reshape(batch, seq, d)
    y_sample = x[n_prompt:n_prompt + n_seq].reshape(n_seq, 1, d)
    return (y_prompt, y_sample, jnp.stack(ks_p), jnp.stack(vs_p), jnp.stack(dn_p), jnp.stack(cv_p),
            jnp.stack(ks_s), jnp.stack(vs_s), jnp.stack(dn_s), jnp.stack(cv_s))
```

```python
import functools
import math

import jax
import jax.numpy as jnp
from jax import lax
from jax.experimental import pallas as pl
from jax.experimental.pallas import tpu as pltpu

F32 = jnp.float32
BF16 = jnp.bfloat16
I32 = jnp.int32
HIGHEST = lax.Precision.HIGHEST

HEAD_DIM = 128
ATT_GROUP = 2
MOBA_BLOCK = 256
MOBA_TOPK = 3
MOBA_Q_SUB = 128
REL_BUCKETS = 32
REL_MAX_DIST = 2048
DN_CHUNK = 64
CONV_W = 4
N_EXPERTS = 16
EXPERTS_PER_GROUP = 4
LN_EPS = 1e-5
NORM_EPS = 1e-6
ROW_TILE = 256
MOE_TILE = 256
ROW_DMA_UNROLL = 8
PAGES_PER_STEP = 16
NEG = -1e30
VMEM_LIMIT_V7X = 56 * 1024 * 1024


def _params(semantics):
    return pltpu.CompilerParams(dimension_semantics=semantics, vmem_limit_bytes=VMEM_LIMIT_V7X)


def _nt_dot(a, b, precision=None):
    return lax.dot_general(a, b, (((1,), (1,)), ((), ())), precision=precision, preferred_element_type=F32)


def _tn_dot(a, b, precision=None):
    return lax.dot_general(a, b, (((0,), (0,)), ((), ())), precision=precision, preferred_element_type=F32)


def _hdot(a, b):
    return jnp.dot(a, b, precision=HIGHEST, preferred_element_type=F32)


def _bdot(a, b):
    return jnp.dot(a.astype(BF16), b.astype(BF16), preferred_element_type=F32)


def _dot3(a, b):
    ah, bh = a.astype(BF16), b.astype(BF16)
    al, bl = (a - ah.astype(F32)).astype(BF16), (b - bh.astype(F32)).astype(BF16)
    dot = functools.partial(jnp.dot, preferred_element_type=F32)
    return dot(ah, bh) + (dot(ah, bl) + dot(al, bh))


def _layer_norm(x, g, b):
    mu = jnp.mean(x, axis=-1, keepdims=True)
    xc = x - mu
    var = jnp.mean(xc * xc, axis=-1, keepdims=True)
    return xc * lax.rsqrt(var + LN_EPS) * g + b


def _silu(x):
    return x * (1.0 / (1.0 + jnp.exp(-x)))


def _sigmoid(x):
    return 1.0 / (1.0 + jnp.exp(-x))


def _softplus(x):
    return jnp.maximum(x, 0.0) + jnp.log(1.0 + jnp.exp(-jnp.abs(x)))


def _rel_bucket(dist):
    n = jnp.maximum(dist, 0)
    max_exact = REL_BUCKETS // 2
    nf = jnp.maximum(n, 1).astype(F32)
    large = max_exact + (jnp.log(nf / max_exact) / math.log(REL_MAX_DIST / max_exact)
                         * (REL_BUCKETS - max_exact)).astype(I32)
    large = jnp.minimum(large, REL_BUCKETS - 1)
    return jnp.where(n < max_exact, n, large)


def _ln_kernel(x_ref, g_ref, b_ref, o_ref):
    o_ref[...] = _layer_norm(x_ref[...], g_ref[...], b_ref[...])


def _input_layer_norm(x, g, b):
    nt, d = x.shape
    return pl.pallas_call(
        _ln_kernel,
        grid=(nt // ROW_TILE,),
        in_specs=[pl.BlockSpec((ROW_TILE, d), lambda i: (i, 0)),
                  pl.BlockSpec((1, d), lambda i: (0, 0)),
                  pl.BlockSpec((1, d), lambda i: (0, 0))],
        out_specs=pl.BlockSpec((ROW_TILE, d), lambda i: (i, 0)),
        out_shape=jax.ShapeDtypeStruct((nt, d), F32),
        compiler_params=_params(("parallel",)),
        name="input_layer_norm",
    )(x, g.reshape(1, d), b.reshape(1, d))


def _inproj_kernel(x_ref, w_ref, *o_refs):
    xb = x_ref[...].astype(BF16)
    off = 0
    for o_ref in o_refs:
        n = o_ref.shape[1]
        o_ref[...] = jnp.dot(xb, w_ref[:, off:off + n], preferred_element_type=F32)
        off += n


def _in_projection(x, w_bf16, widths):
    nt, d = x.shape
    cols = w_bf16.shape[1]
    assert sum(widths) == cols
    return pl.pallas_call(
        _inproj_kernel,
        grid=(nt // ROW_TILE,),
        in_specs=[pl.BlockSpec((ROW_TILE, d), lambda i: (i, 0)),
                  pl.BlockSpec((d, cols), lambda i: (0, 0), pipeline_mode=pl.Buffered(1))],
        out_specs=[pl.BlockSpec((ROW_TILE, n), lambda i: (i, 0)) for n in widths],
        out_shape=[jax.ShapeDtypeStruct((nt, n), F32) for n in widths],
        compiler_params=_params(("parallel",)),
        name="in_projection",
    )(x, w_bf16)


def _bias_tile_kernel(tab_ref, o_ref):
    h = pl.program_id(0)
    d = pl.program_id(1)
    r = lax.broadcasted_iota(I32, (MOBA_BLOCK, MOBA_BLOCK), 0)
    c = lax.broadcasted_iota(I32, (MOBA_BLOCK, MOBA_BLOCK), 1)
    dist = d * MOBA_BLOCK + r - c
    bucket = _rel_bucket(dist)
    acc = jnp.zeros((MOBA_BLOCK, MOBA_BLOCK), F32)
    for k in range(REL_BUCKETS):
        acc = jnp.where(bucket == k, tab_ref[k, h], acc)
    o_ref[0, 0] = jnp.where(dist >= 0, acc, NEG)


def _bias_tiles(rel_bias, n_blk):
    heads = rel_bias.shape[1]
    return pl.pallas_call(
        _bias_tile_kernel,
        grid=(heads, n_blk),
        in_specs=[pl.BlockSpec(memory_space=pltpu.SMEM)],
        out_specs=pl.BlockSpec((1, 1, MOBA_BLOCK, MOBA_BLOCK), lambda h, d: (h, d, 0, 0)),
        out_shape=jax.ShapeDtypeStruct((heads, n_blk, MOBA_BLOCK, MOBA_BLOCK), F32),
        compiler_params=_params(("parallel", "parallel")),
        name="bias_tiles",
    )(rel_bias.astype(F32))


def _moba_prompt_kernel(q_ref, k_ref, v_ref, bias_ref, o_ref, kaug_ref, vb_ref, kmean_ref):
    i = pl.program_id(2)
    seq = k_ref.shape[0]
    n_blk = seq // MOBA_BLOCK
    scale = HEAD_DIM ** -0.5

    @pl.when(i == 0)
    def _():
        kaug_ref[:, :HEAD_DIM] = k_ref[...].astype(BF16)
        rows = lax.broadcasted_iota(I32, (seq, HEAD_DIM), 0) // MOBA_BLOCK
        lanes = lax.broadcasted_iota(I32, (seq, HEAD_DIM), 1)
        kaug_ref[:, HEAD_DIM:] = jnp.where(rows == lanes, 1.0, 0.0).astype(BF16)
        vb_ref[...] = v_ref[...].astype(BF16)
        for j in range(n_blk):
            kmean_ref[j:j + 1, :] = jnp.mean(k_ref[j * MOBA_BLOCK:(j + 1) * MOBA_BLOCK, :], axis=0, keepdims=True)

    tq = q_ref.shape[0]
    kmean = kmean_ref[...].astype(BF16)
    chains = []
    for hh in range(ATT_GROUP):
        q = q_ref[:, hh * HEAD_DIM:(hh + 1) * HEAD_DIM]
        gate = _nt_dot(kmean, q.astype(BF16))
        jidx = lax.broadcasted_iota(I32, gate.shape, 0)
        rank = jnp.zeros(gate.shape, I32)
        for jp in range(n_blk):
            row = gate[jp:jp + 1, :]
            before = (row > gate) | ((row == gate) & (jp < jidx))
            rank = rank + jnp.where(before, jnp.where(jp < i, 1, 0), 0)
        sel = ((jidx < i) & (rank < MOBA_TOPK)) | (jidx == i)
        selneg = jnp.where(sel, 0.0, NEG)
        selneg = jnp.concatenate([selneg, jnp.zeros((HEAD_DIM - n_blk, tq), F32)], axis=0)
        q_aug = jnp.concatenate([q.astype(BF16), selneg.T.astype(BF16)], axis=1)
        for r0 in range(0, tq, MOBA_Q_SUB):
            chains.append((hh, r0, q_aug[r0:r0 + MOBA_Q_SUB]))

    def body(t, carry):
        start = pl.multiple_of((i - t) * MOBA_BLOCK, MOBA_BLOCK)
        kj = kaug_ref[pl.ds(start, MOBA_BLOCK), :]
        vj = vb_ref[pl.ds(start, MOBA_BLOCK), :]
        ss = [_nt_dot(qc, kj) * scale + bias_ref[hh, t, r0:r0 + MOBA_Q_SUB, :] for hh, r0, qc in chains]
        m_new = [jnp.maximum(c[0], jnp.max(s, axis=1, keepdims=True)) for c, s in zip(carry, ss)]
        ps = [jnp.exp(s - mn) for s, mn in zip(ss, m_new)]
        alphas = [jnp.exp(c[0] - mn) for c, mn in zip(carry, m_new)]
        pvs = [jnp.dot(p.astype(BF16), vj, preferred_element_type=F32) for p in ps]
        ls = [a * c[1] + jnp.sum(p, axis=1, keepdims=True) for a, c, p in zip(alphas, carry, ps)]
        accs = [a * c[2] + pv for a, c, pv in zip(alphas, carry, pvs)]
        return tuple(zip(m_new, ls, accs))

    init = tuple((jnp.full((MOBA_Q_SUB, 1), -jnp.inf, F32), jnp.zeros((MOBA_Q_SUB, 1), F32),
                  jnp.zeros((MOBA_Q_SUB, HEAD_DIM), F32)) for _ in chains)
    final = lax.fori_loop(0, i + 1, body, init)
    for (hh, r0, _), (_, l, acc) in zip(chains, final):
        o_ref[r0:r0 + MOBA_Q_SUB, hh * HEAD_DIM:(hh + 1) * HEAD_DIM] = (acc / l).astype(o_ref.dtype)


def _moba_prompt(qa, ka, va, bias_tiles, batch, seq):
    kvh = ka.shape[1] // HEAD_DIM
    group_w = ATT_GROUP * HEAD_DIM
    n_blk = seq // MOBA_BLOCK
    assert seq % MOBA_BLOCK == 0 and n_blk <= HEAD_DIM
    return pl.pallas_call(
        _moba_prompt_kernel,
        grid=(kvh, batch, n_blk),
        in_specs=[pl.BlockSpec((MOBA_BLOCK, group_w), lambda g, b, i: (b * n_blk + i, g)),
                  pl.BlockSpec((seq, HEAD_DIM), lambda g, b, i: (b, g)),
                  pl.BlockSpec((seq, HEAD_DIM), lambda g, b, i: (b, g)),
                  pl.BlockSpec((ATT_GROUP, n_blk, MOBA_BLOCK, MOBA_BLOCK), lambda g, b, i: (g, 0, 0, 0))],
        out_specs=pl.BlockSpec((MOBA_BLOCK, group_w), lambda g, b, i: (b * n_blk + i, g)),
        out_shape=jax.ShapeDtypeStruct((batch * seq, kvh * group_w), BF16),
        scratch_shapes=[pltpu.VMEM((seq, 2 * HEAD_DIM), BF16),
                        pltpu.VMEM((seq, HEAD_DIM), BF16),
                        pltpu.VMEM((n_blk, HEAD_DIM), F32)],
        compiler_params=_params(("parallel", "parallel", "arbitrary")),
        name="moba_prompt",
    )(qa, ka, va, bias_tiles)


def _page_sum_kernel(pt_ref, *refs):
    o_ref = refs[-1]
    for u in range(0, PAGES_PER_STEP, 2):
        o_ref[0, u // 2] = jnp.sum(refs[u][0, 0], axis=0) + jnp.sum(refs[u + 1][0, 0], axis=0)


def _past_block_sums(cache, layer, page_table):
    n_seq, n_pages = page_table.shape
    page, kvh = cache.shape[2], cache.shape[3]
    assert 2 * page == MOBA_BLOCK and n_pages % PAGES_PER_STEP == 0
    steps = n_pages // PAGES_PER_STEP
    blk_per_step = PAGES_PER_STEP // 2

    def page_spec(u):
        return pl.BlockSpec((1, 1, page, kvh, HEAD_DIM),
                            lambda s, t, pt: (layer, pt[s, t * PAGES_PER_STEP + u], 0, 0, 0))

    return pl.pallas_call(
        _page_sum_kernel,
        grid_spec=pltpu.PrefetchScalarGridSpec(
            num_scalar_prefetch=1,
            grid=(n_seq, steps),
            in_specs=[page_spec(u) for u in range(PAGES_PER_STEP)],
            out_specs=pl.BlockSpec((1, blk_per_step, kvh, HEAD_DIM), lambda s, t, pt: (s, t, 0, 0))),
        out_shape=jax.ShapeDtypeStruct((n_seq, n_pages // 2, kvh, HEAD_DIM), F32),
        compiler_params=_params(("parallel", "arbitrary")),
        name="past_block_sums",
    )(page_table, *([cache] * PAGES_PER_STEP))


def _decode_select_kernel(q_ref, ksum_ref, o_ref):
    heads = q_ref.shape[1]
    n_blk = ksum_ref.shape[2]
    q = q_ref[0]
    qb = jnp.concatenate([q, q], axis=0).astype(BF16)
    hrow = lax.broadcasted_iota(I32, (heads, n_blk), 0)
    gate = jnp.zeros((heads, n_blk), F32)
    for g in range(heads // ATT_GROUP):
        kmean = ksum_ref[0, g] * (1.0 / MOBA_BLOCK)
        gate = jnp.where(hrow // ATT_GROUP == g, _nt_dot(qb, kmean.astype(BF16))[:heads], gate)
    nidx = lax.broadcasted_iota(I32, (heads, n_blk), 1)
    rank = jnp.zeros((heads, n_blk), I32)
    for n in range(n_blk):
        col = gate[:, n:n + 1]
        rank = rank + jnp.where((col > gate) | ((col == gate) & (n < nidx)), 1, 0)
    lane = lax.broadcasted_iota(I32, (heads, HEAD_DIM), 1)
    out = jnp.zeros((heads, HEAD_DIM), I32)
    for r in range(MOBA_TOPK):
        idx = jnp.sum(jnp.where(rank == r, nidx.astype(F32), 0.0), axis=1, keepdims=True)
        out = jnp.where(lane == r, idx.astype(I32), out)
    o_ref[0] = out


def _decode_select(q3, ksum):
    n_seq, heads, _ = q3.shape
    kvh, n_blk = ksum.shape[1], ksum.shape[2]
    return pl.pallas_call(
        _decode_select_kernel,
        grid=(n_seq,),
        in_specs=[pl.BlockSpec((1, heads, HEAD_DIM), lambda s: (s, 0, 0)),
                  pl.BlockSpec((1, kvh, n_blk, HEAD_DIM), lambda s: (s, 0, 0, 0))],
        out_specs=pl.BlockSpec((1, heads, HEAD_DIM), lambda s: (s, 0, 0)),
        out_shape=jax.ShapeDtypeStruct((n_seq, heads, HEAD_DIM), I32),
        compiler_params=_params(("parallel",)),
        name="decode_select",
    )(q3, ksum)


def _decode_attend_kernel(pt_ref, blk_ref, q_ref, kn_ref, vn_ref, tab_ref, ck_hbm, cv_hbm, o_ref,
                          kbuf, vbuf, sems, *, past_len, layer):
    n_tiles = 2 * MOBA_TOPK
    s = pl.program_id(0)
    heads = q_ref.shape[1]
    page = kbuf.shape[2]
    scale = HEAD_DIM ** -0.5
    rows = 16
    bround = lambda a: a.astype(BF16).astype(F32)
    lane = lax.broadcasted_iota(I32, (1, page), 1)

    def tile_copies(h):
        g = h // ATT_GROUP
        copies = []
        for u in range(n_tiles):
            pg = pt_ref[s, 2 * blk_ref[s * heads + h, u // 2] + (u % 2)]
            copies.append(pltpu.make_async_copy(ck_hbm.at[layer, pg, :, g, :], kbuf.at[h, u], sems.at[h]))
            copies.append(pltpu.make_async_copy(cv_hbm.at[layer, pg, :, g, :], vbuf.at[h, u], sems.at[h]))
        return copies

    for h in range(heads):
        for cp in tile_copies(h):
            cp.start()

    for h in range(heads):
        for cp in tile_copies(h):
            cp.wait()
        q = jnp.broadcast_to(q_ref[0, h:h + 1, :], (rows, HEAD_DIM)).astype(BF16)
        logits = []
        for u in range(n_tiles):
            kpos = blk_ref[s * heads + h, u // 2] * MOBA_BLOCK + (u % 2) * page + lane
            bucket = _rel_bucket(past_len - kpos)
            bias = jnp.zeros((1, page), F32)
            for k in range(REL_BUCKETS):
                bias = jnp.where(bucket == k, tab_ref[k, h], bias)
            logits.append(_nt_dot(q, kbuf[h, u].astype(BF16))[0:1, :] * scale + bias)
        g = h // ATT_GROUP
        kn = kn_ref[0, g:g + 1, :]
        vn = vn_ref[0, g:g + 1, :]
        own = jnp.sum(q[0:1, :].astype(F32) * bround(kn), axis=1, keepdims=True) * scale + tab_ref[0, h]

        m = own
        for lg in logits:
            m = jnp.maximum(m, jnp.max(lg, axis=1, keepdims=True))
        e_own = jnp.exp(own - m)
        es = [jnp.exp(lg - m) for lg in logits]
        den = e_own
        for e in es:
            den = den + jnp.sum(e, axis=1, keepdims=True)
        acc = bround(e_own / den) * bround(vn)
        for u in range(n_tiles):
            p = jnp.broadcast_to(es[u] / den, (rows, page)).astype(BF16)
            acc = acc + jnp.dot(p, vbuf[h, u].astype(BF16), preferred_element_type=F32)[0:1, :]
        o_ref[0, h:h + 1, :] = acc


def _decode_attend(q3, kn3, vn3, cache_k, cache_v, layer, page_table, blocks, rel_bias, past_len):
    n_seq, heads, _ = q3.shape
    kvh = kn3.shape[1]
    page = cache_k.shape[2]
    n_tiles = 2 * MOBA_TOPK
    return pl.pallas_call(
        functools.partial(_decode_attend_kernel, past_len=past_len, layer=layer),
        grid_spec=pltpu.PrefetchScalarGridSpec(
            num_scalar_prefetch=2,
            grid=(n_seq,),
            in_specs=[pl.BlockSpec((1, heads, HEAD_DIM), lambda s, pt, bk: (s, 0, 0)),
                      pl.BlockSpec((1, kvh, HEAD_DIM), lambda s, pt, bk: (s, 0, 0)),
                      pl.BlockSpec((1, kvh, HEAD_DIM), lambda s, pt, bk: (s, 0, 0)),
                      pl.BlockSpec(memory_space=pltpu.SMEM),
                      pl.BlockSpec(memory_space=pl.ANY),
                      pl.BlockSpec(memory_space=pl.ANY)],
            out_specs=pl.BlockSpec((1, heads, HEAD_DIM), lambda s, pt, bk: (s, 0, 0)),
            scratch_shapes=[pltpu.VMEM((heads, n_tiles, page, HEAD_DIM), F32),
                            pltpu.VMEM((heads, n_tiles, page, HEAD_DIM), F32),
                            pltpu.SemaphoreType.DMA((heads,))]),
        out_shape=jax.ShapeDtypeStruct((n_seq, heads, HEAD_DIM), F32),
        compiler_params=_params(("arbitrary",)),
        name="decode_attend",
    )(page_table, blocks.reshape(n_seq * heads, MOBA_TOPK), q3, kn3, vn3, rel_bias.astype(F32), cache_k, cache_v)


def _conv_gate_math(bg, al, dtb):
    lane = lax.broadcasted_iota(I32, bg.shape, 1)
    return jnp.where(lane < 8, _sigmoid(bg), -jnp.exp(al) * _softplus(bg + dtb))


def _head_l2norm(x, n_heads, mult):
    outs = []
    for h in range(n_heads):
        seg = x[:, h * HEAD_DIM:(h + 1) * HEAD_DIM]
        ss = jnp.sum(seg * seg, axis=1, keepdims=True)
        outs.append(seg * (lax.rsqrt(ss + NORM_EPS) * mult))
    return jnp.concatenate(outs, axis=1)


def _dn_prep_kernel(cur_ref, prev_ref, bg_ref, cw_ref, al_ref, dtb_ref, qn_ref, kn_ref, vv_ref, gb_ref):
    i = pl.program_id(1)
    t, width3 = cur_ref.shape
    width = width3 // 3
    n_heads = width // HEAD_DIM
    cur = cur_ref[...]
    prev = prev_ref[...] * jnp.where(i > 0, 1.0, 0.0)
    row8 = lax.broadcasted_iota(I32, (8, 1), 0)
    y = cur * cw_ref[CONV_W - 1:CONV_W, :]
    for k in range(1, CONV_W):
        rolled = pltpu.roll(cur, k, 0)
        head = jnp.where(row8 < k, pltpu.roll(prev, k, 0), rolled[:8])
        shifted = jnp.concatenate([head, rolled[8:]], axis=0)
        y = y + shifted * cw_ref[CONV_W - 1 - k:CONV_W - k, :]
    a = _silu(y)
    qn_ref[...] = _head_l2norm(a[:, :width], n_heads, HEAD_DIM ** -0.5)
    kn_ref[...] = _head_l2norm(a[:, width:2 * width], n_heads, 1.0)
    vv_ref[...] = a[:, 2 * width:]
    gb_ref[...] = _conv_gate_math(bg_ref[...], al_ref[...], dtb_ref[...])


def _dn_prep(qkvb, bg, conv_w, al_row, dtb_row, batch, seq):
    width3 = qkvb.shape[1]
    width = width3 // 3
    tiles = seq // ROW_TILE
    rows = batch * seq
    row_spec = lambda n: pl.BlockSpec((ROW_TILE, n), lambda b, i: (b * tiles + i, 0))
    return pl.pallas_call(
        _dn_prep_kernel,
        grid=(batch, tiles),
        in_specs=[row_spec(width3),
                  pl.BlockSpec((8, width3), lambda b, i: (jnp.maximum((b * tiles + i) * (ROW_TILE // 8) - 1, 0), 0)),
                  row_spec(HEAD_DIM),
                  pl.BlockSpec((CONV_W, width3), lambda b, i: (0, 0)),
                  pl.BlockSpec((1, HEAD_DIM), lambda b, i: (0, 0)),
                  pl.BlockSpec((1, HEAD_DIM), lambda b, i: (0, 0))],
        out_specs=[row_spec(width), row_spec(width), row_spec(width), row_spec(HEAD_DIM)],
        out_shape=[jax.ShapeDtypeStruct((rows, width), F32)] * 3 + [jax.ShapeDtypeStruct((rows, HEAD_DIM), F32)],
        compiler_params=_params(("parallel", "parallel")),
        name="dn_prep",
    )(qkvb, qkvb, bg, conv_w, al_row, dtb_row)


def _dn_chunk_kernel(qn_ref, kn_ref, vv_ref, gb_ref, og_ref, nw_ref, y_ref, s_out_ref, s_ref):
    c = pl.program_id(1)
    n_chunks = pl.num_programs(1)
    ch, width = qn_ref.shape
    n_heads = width // HEAD_DIM

    @pl.when(c == 0)
    def _():
        s_ref[...] = jnp.zeros(s_ref.shape, F32)

    ri = lax.broadcasted_iota(I32, (ch, ch), 0)
    ci = lax.broadcasted_iota(I32, (ch, ch), 1)
    tri = ri >= ci
    strict = ri > ci
    eye = ri == ci
    gbv = gb_ref[...]
    gcum_all = _hdot(jnp.where(tri, 1.0, 0.0), gbv)

    hs = range(n_heads)
    sls = [slice(h * HEAD_DIM, (h + 1) * HEAD_DIM) for h in hs]
    dot = functools.partial(jnp.dot, preferred_element_type=F32)
    q = [qn_ref[:, sl] for sl in sls]
    k = [kn_ref[:, sl] for sl in sls]
    beta = [gbv[:, h:h + 1] for h in hs]
    gc = [gcum_all[:, 8 + h:9 + h] for h in hs]
    gcb = [jnp.broadcast_to(g, (ch, ch)) for g in gc]
    grow = [jnp.sum(jnp.where(eye, g, 0.0), axis=0, keepdims=True) for g in gcb]
    decay = [jnp.where(tri, jnp.exp(jnp.where(tri, g - r, 0.0)), 0.0) for g, r in zip(gcb, grow)]
    kbeta = [kk * b for kk, b in zip(k, beta)]
    egc = [jnp.exp(g) for g in gc]
    k16 = [kk.astype(BF16) for kk in k]
    a = [jnp.where(strict, _nt_dot(kb.astype(BF16), kk) * d, 0.0) for kb, kk, d in zip(kbeta, k16, decay)]
    x = [jnp.concatenate([vv_ref[:, sl] * b, kb * e], axis=1) for sl, b, kb, e in zip(sls, beta, kbeta, egc)]
    npow = [-aa for aa in a]
    span = 1
    while True:
        mm = _dot3 if span <= 2 else _bdot
        x = [xx + mm(n, xx) for n, xx in zip(npow, x)]
        span *= 2
        if span >= ch:
            break
        npow = [mm(n, n) for n in npow]
    s = [s_ref[h] for h in hs]
    s16 = [ss.astype(BF16) for ss in s]
    v_new = [xx[:, :HEAD_DIM] - dot(xx[:, HEAD_DIM:].astype(BF16), ss) for xx, ss in zip(x, s16)]
    v16 = [vn.astype(BF16) for vn in v_new]
    attn = [jnp.where(tri, _nt_dot(qq.astype(BF16), kk) * d, 0.0) for qq, kk, d in zip(q, k16, decay)]
    o = [dot((qq * e).astype(BF16), ss) + dot(at.astype(BF16), vn)
         for qq, e, ss, at, vn in zip(q, egc, s16, attn, v16)]
    g_last = [g[ch - 1:ch, :] for g in gc]
    kdec = [(kk * jnp.exp(gl - g)).astype(BF16) for kk, gl, g in zip(k, g_last, gc)]
    s_new = [ss * jnp.exp(gl) + _tn_dot(kd, vn) for ss, gl, kd, vn in zip(s, g_last, kdec, v16)]
    for h in hs:
        s_ref[h] = s_new[h]
    for h, sl in enumerate(sls):
        on = o[h] * lax.rsqrt(jnp.mean(o[h] * o[h], axis=1, keepdims=True) + NORM_EPS) * nw_ref[...]
        y_ref[:, sl] = (on * _silu(og_ref[:, sl])).astype(y_ref.dtype)

    @pl.when(c == n_chunks - 1)
    def _():
        s_out_ref[0] = s_ref[...]


def _dn_chunks(qn, kn, vv, gb, og, norm_w, batch, seq):
    width = qn.shape[1]
    n_heads = width // HEAD_DIM
    assert seq % DN_CHUNK == 0
    n_chunks = seq // DN_CHUNK
    row_spec = lambda n: pl.BlockSpec((DN_CHUNK, n), lambda b, c: (b * n_chunks + c, 0))
    return pl.pallas_call(
        _dn_chunk_kernel,
        grid=(batch, n_chunks),
        in_specs=[row_spec(width), row_spec(width), row_spec(width), row_spec(HEAD_DIM), row_spec(width),
                  pl.BlockSpec((1, HEAD_DIM), lambda b, c: (0, 0))],
        out_specs=[row_spec(width),
                   pl.BlockSpec((1, n_heads, HEAD_DIM, HEAD_DIM), lambda b, c: (b, 0, 0, 0))],
        out_shape=[jax.ShapeDtypeStruct((batch * seq, width), BF16),
                   jax.ShapeDtypeStruct((batch, n_heads, HEAD_DIM, HEAD_DIM), F32)],
        scratch_shapes=[pltpu.VMEM((n_heads, HEAD_DIM, HEAD_DIM), F32)],
        compiler_params=_params(("parallel", "arbitrary")),
        name="dn_chunks",
    )(qn, kn, vv, gb, og, norm_w)


def _dn_step_kernel(x_ref, cs_ref, bg_ref, og_ref, s_ref, cw_ref, al_ref, dtb_ref, nw_ref,
                    y_ref, s_out_ref, cs_out_ref):
    width3 = x_ref.shape[2]
    width = width3 // 3
    n_heads = width // HEAD_DIM
    xn = x_ref[0]
    cs = cs_ref[0]
    y = xn * cw_ref[CONV_W - 1:CONV_W, :]
    for j in range(CONV_W - 1):
        y = y + cs[j:j + 1, :] * cw_ref[j:j + 1, :]
    for j in range(1, CONV_W - 1):
        cs_out_ref[0, j - 1:j, :] = cs_ref[0, j:j + 1, :]
    cs_out_ref[0, CONV_W - 2:CONV_W - 1, :] = xn
    a = _silu(y)
    qn = _head_l2norm(a[:, :width], n_heads, HEAD_DIM ** -0.5)
    kn = _head_l2norm(a[:, width:2 * width], n_heads, 1.0)
    vv = a[:, 2 * width:]
    gbv = _conv_gate_math(bg_ref[0], al_ref[...], dtb_ref[...])
    og = og_ref[0]
    rows = 16
    rowi = lax.broadcasted_iota(I32, (rows, HEAD_DIM), 0)
    bround = lambda t: t.astype(BF16).astype(F32)
    for h in range(n_heads):
        sl = slice(h * HEAD_DIM, (h + 1) * HEAD_DIM)
        q, k, v = qn[:, sl], kn[:, sl], vv[:, sl]
        beta = gbv[:, h:h + 1]
        eg = jnp.exp(gbv[:, 8 + h:9 + h])
        s = s_ref[0, h]
        lhs = jnp.where(rowi == 0, k * beta * eg, jnp.where(rowi == 1, q * eg, 0.0))
        prod = jnp.dot(lhs.astype(BF16), s.astype(BF16), preferred_element_type=F32)
        v_new = v * beta - prod[0:1, :]
        attn = jnp.sum(bround(q) * bround(k), axis=1, keepdims=True)
        o = prod[1:2, :] + bround(attn) * bround(v_new)
        outer = _tn_dot(jnp.where(rowi == 0, k, 0.0).astype(BF16),
                        jnp.broadcast_to(v_new, (rows, HEAD_DIM)).astype(BF16))
        s_out_ref[0, h] = s * eg + outer
        o = o * lax.rsqrt(jnp.mean(o * o, axis=1, keepdims=True) + NORM_EPS) * nw_ref[...]
        y_ref[0, :, sl] = o * _silu(og[:, sl])


def _dn_step(x3, conv_state, bg3, og3, dn_state, conv_w, al_row, dtb_row, norm_w):
    n_seq, _, width3 = x3.shape
    width = width3 // 3
    n_heads = width // HEAD_DIM
    vec = lambda n: pl.BlockSpec((1, 1, n), lambda s: (s, 0, 0))
    const = lambda r, n: pl.BlockSpec((r, n), lambda s: (0, 0))
    return pl.pallas_call(
        _dn_step_kernel,
        grid=(n_seq,),
        in_specs=[vec(width3),
                  pl.BlockSpec((1, CONV_W - 1, width3), lambda s: (s, 0, 0)),
                  vec(HEAD_DIM), vec(width),
                  pl.BlockSpec((1, n_heads, HEAD_DIM, HEAD_DIM), lambda s: (s, 0, 0, 0)),
                  const(CONV_W, width3), const(1, HEAD_DIM), const(1, HEAD_DIM), const(1, HEAD_DIM)],
        out_specs=[vec(width),
                   pl.BlockSpec((1, n_heads, HEAD_DIM, HEAD_DIM), lambda s: (s, 0, 0, 0)),
                   pl.BlockSpec((1, CONV_W - 1, width3), lambda s: (s, 0, 0))],
        out_shape=[jax.ShapeDtypeStruct((n_seq, 1, width), F32),
                   jax.ShapeDtypeStruct(dn_state.shape, F32),
                   jax.ShapeDtypeStruct(conv_state.shape, F32)],
        compiler_params=_params(("parallel",)),
        name="dn_step",
    )(x3, conv_state, bg3, og3, dn_state, conv_w, al_row, dtb_row, norm_w)


def _top2_sum(a, b, c, d):
    hi1, lo1 = jnp.maximum(a, b), jnp.minimum(a, b)
    hi2, lo2 = jnp.maximum(c, d), jnp.minimum(c, d)
    return jnp.maximum(hi1, hi2) + jnp.maximum(jnp.minimum(hi1, hi2), jnp.maximum(lo1, lo2))


def _outproj_router_kernel(y_ref, x_ref, w_ref, g_ref, b_ref, wr_ref, rb_ref,
                           x1_ref, info_ref, infot_ref, cnt_ref, carry_ref, *, alpha, n_valid):
    i = pl.program_id(0)
    tm = x_ref.shape[0]

    @pl.when(i == 0)
    def _():
        carry_ref[...] = jnp.zeros(carry_ref.shape, F32)

    mixed = jnp.dot(y_ref[...], w_ref[...], preferred_element_type=F32)
    x1 = _layer_norm(alpha * x_ref[...] + mixed, g_ref[...], b_ref[...])
    x1_ref[...] = x1

    scores = _sigmoid(_nt_dot(wr_ref[...], x1.astype(BF16)))
    selv = scores + rb_ref[:, 0:1]
    sel = [selv[e:e + 1, :] for e in range(N_EXPERTS)]
    sc = [scores[e:e + 1, :] for e in range(N_EXPERTS)]
    n_groups = N_EXPERTS // EXPERTS_PER_GROUP
    gs = [_top2_sum(*sel[EXPERTS_PER_GROUP * g:EXPERTS_PER_GROUP * (g + 1)]) for g in range(n_groups)]
    best, bidx = gs[0], jnp.zeros((1, tm), I32)
    for g in range(1, n_groups):
        better = gs[g] > best
        best = jnp.where(better, gs[g], best)
        bidx = jnp.where(better, g, bidx)
    ninf = jnp.full((1, tm), -jnp.inf, F32)
    vals = [jnp.where(bidx == e // EXPERTS_PER_GROUP, sel[e], ninf) for e in range(N_EXPERTS)]
    b1, i1, s1 = vals[0], jnp.zeros((1, tm), I32), sc[0]
    for e in range(1, N_EXPERTS):
        better = vals[e] > b1
        b1 = jnp.where(better, vals[e], b1)
        i1 = jnp.where(better, e, i1)
        s1 = jnp.where(better, sc[e], s1)
    b2, i2, s2 = ninf, jnp.zeros((1, tm), I32), jnp.zeros((1, tm), F32)
    for e in range(N_EXPERTS):
        cand = jnp.where(i1 == e, ninf, vals[e])
        better = cand > b2
        b2 = jnp.where(better, cand, b2)
        i2 = jnp.where(better, e, i2)
        s2 = jnp.where(better, sc[e], s2)
    tok = i * tm + lax.broadcasted_iota(I32, (1, tm), 1)
    valid = tok < n_valid
    inv = 1.0 / (s1 + s2)
    g1 = jnp.where(valid, s1 * inv, 0.0)
    g2 = jnp.where(valid, s2 * inv, 0.0)

    erow = lax.broadcasted_iota(I32, (N_EXPERTS, tm), 0)
    oh1 = jnp.where((erow == i1) & valid, 1.0, 0.0)
    oh2 = jnp.where((erow == i2) & valid, 1.0, 0.0)
    both = oh1 + oh2
    upper = jnp.where(lax.broadcasted_iota(I32, (tm, tm), 0) <= lax.broadcasted_iota(I32, (tm, tm), 1), 1.0, 0.0)
    incl = jnp.dot(both.astype(BF16), upper.astype(BF16), preferred_element_type=F32)
    before = carry_ref[:, 0:1] + incl - both
    r1 = jnp.sum(oh1 * before, axis=0, keepdims=True)
    r2 = jnp.sum(oh2 * before, axis=0, keepdims=True)
    carry_ref[...] = carry_ref[...] + incl[:, tm - 1:tm]
    cnt_ref[...] = carry_ref[...]

    row8 = lax.broadcasted_iota(I32, (8, tm), 0)
    info = jnp.zeros((8, tm), F32)
    for r, val in enumerate((i1.astype(F32), i2.astype(F32), g1, g2, r1, r2)):
        info = jnp.where(row8 == r, val, info)
    info_ref[...] = info
    infot_ref[...] = jnp.concatenate([info, jnp.zeros((HEAD_DIM - 8, tm), F32)], axis=0).T


def _outproj_router(y, x, w_out_bf16, ln_g, ln_b, wr_t, rb_col, alpha, n_valid):
    nt, d = x.shape
    dm = y.shape[1]
    kern = functools.partial(_outproj_router_kernel, alpha=alpha, n_valid=n_valid)
    return pl.pallas_call(
        kern,
        grid=(nt // ROW_TILE,),
        in_specs=[pl.BlockSpec((ROW_TILE, dm), lambda i: (i, 0)),
                  pl.BlockSpec((ROW_TILE, d), lambda i: (i, 0)),
                  pl.BlockSpec((dm, d), lambda i: (0, 0)),
                  pl.BlockSpec((1, d), lambda i: (0, 0)),
                  pl.BlockSpec((1, d), lambda i: (0, 0)),
                  pl.BlockSpec((N_EXPERTS, d), lambda i: (0, 0)),
                  pl.BlockSpec((N_EXPERTS, HEAD_DIM), lambda i: (0, 0))],
        out_specs=[pl.BlockSpec((ROW_TILE, d), lambda i: (i, 0)),
                   pl.BlockSpec((8, ROW_TILE), lambda i: (0, i)),
                   pl.BlockSpec((ROW_TILE, HEAD_DIM), lambda i: (i, 0)),
                   pl.BlockSpec((N_EXPERTS, HEAD_DIM), lambda i: (0, 0))],
        out_shape=[jax.ShapeDtypeStruct((nt, d), F32),
                   jax.ShapeDtypeStruct((8, nt), F32),
                   jax.ShapeDtypeStruct((nt, HEAD_DIM), F32),
                   jax.ShapeDtypeStruct((N_EXPERTS, HEAD_DIM), F32)],
        scratch_shapes=[pltpu.VMEM((N_EXPERTS, HEAD_DIM), F32)],
        compiler_params=_params(("arbitrary",)),
        name="outproj_router",
    )(y, x, w_out_bf16, ln_g, ln_b, wr_t, rb_col)


def _row_copy(src_hbm, src_row, dst_ref, dst_row, sem):
    return pltpu.make_async_copy(src_hbm.at[pl.ds(src_row, 1), :], dst_ref.at[pl.ds(dst_row, 1), :], sem)


def _moe_gather_kernel(rt_ref, nu_ref, x_hbm, o_ref, sem):
    m = pl.program_id(0)
    tm = o_ref.shape[0]

    @pl.when(m < nu_ref[0])
    def _():
        def issue(r, carry):
            _row_copy(x_hbm, rt_ref[m * tm + r], o_ref, r, sem).start()
            return carry

        lax.fori_loop(0, tm, issue, 0, unroll=ROW_DMA_UNROLL)

        def drain(r, carry):
            _row_copy(x_hbm, 0, o_ref, r, sem).wait()
            return carry

        lax.fori_loop(0, tm, drain, 0, unroll=ROW_DMA_UNROLL)

    @pl.when(m >= nu_ref[0])
    def _():
        o_ref[...] = jnp.zeros(o_ref.shape, o_ref.dtype)


def _moe_gather(row_tok, n_used, x):
    d = x.shape[1]
    n_rows = row_tok.shape[0]
    return pl.pallas_call(
        _moe_gather_kernel,
        grid_spec=pltpu.PrefetchScalarGridSpec(
            num_scalar_prefetch=2,
            grid=(n_rows // MOE_TILE,),
            in_specs=[pl.BlockSpec(memory_space=pl.ANY)],
            out_specs=pl.BlockSpec((MOE_TILE, d), lambda m, rt, nu: (m, 0)),
            scratch_shapes=[pltpu.SemaphoreType.DMA(())]),
        out_shape=jax.ShapeDtypeStruct((n_rows, d), x.dtype),
        compiler_params=_params(("arbitrary",)),
        name="moe_gather",
    )(row_tok, n_used, x)


def _moe_expert_kernel(be_ref, nu_ref, x_ref, wg_ref, wu_ref, wd_ref, o_ref):
    m = pl.program_id(0)

    @pl.when(m < nu_ref[0])
    def _():
        xb = x_ref[...].astype(BF16)
        hid = _silu(jnp.dot(xb, wg_ref[0], preferred_element_type=F32)) * jnp.dot(
            xb, wu_ref[0], preferred_element_type=F32)
        o_ref[...] = jnp.dot(hid.astype(BF16), wd_ref[0], preferred_element_type=F32)

    @pl.when(m >= nu_ref[0])
    def _():
        o_ref[...] = jnp.zeros(o_ref.shape, o_ref.dtype)


def _moe_experts(blk_e, n_used, xs, wg, wu, wd):
    n_rows, d = xs.shape
    f = wg.shape[2]
    last = lambda m, nu: jnp.minimum(m, nu[0] - 1)
    return pl.pallas_call(
        _moe_expert_kernel,
        grid_spec=pltpu.PrefetchScalarGridSpec(
            num_scalar_prefetch=2,
            grid=(n_rows // MOE_TILE,),
            in_specs=[pl.BlockSpec((MOE_TILE, d), lambda m, be, nu: (last(m, nu), 0)),
                      pl.BlockSpec((1, d, f), lambda m, be, nu: (be[m], 0, 0)),
                      pl.BlockSpec((1, d, f), lambda m, be, nu: (be[m], 0, 0)),
                      pl.BlockSpec((1, f, d), lambda m, be, nu: (be[m], 0, 0))],
            out_specs=pl.BlockSpec((MOE_TILE, d), lambda m, be, nu: (m, 0))),
        out_shape=jax.ShapeDtypeStruct((n_rows, d), F32),
        compiler_params=_params(("arbitrary",)),
        name="moe_experts",
    )(blk_e, n_used, xs, wg, wu, wd)


def _moe_combine_kernel(dest_ref, y_hbm, x_ref, it_ref, g_ref, b_ref, o_ref, buf_ref, sem, *, alpha, nt):
    i = pl.program_id(0)
    tm = x_ref.shape[0]

    def issue(r, carry):
        _row_copy(y_hbm, dest_ref[i * tm + r], buf_ref.at[0], r, sem).start()
        _row_copy(y_hbm, dest_ref[nt + i * tm + r], buf_ref.at[1], r, sem).start()
        return carry

    lax.fori_loop(0, tm, issue, 0, unroll=ROW_DMA_UNROLL)

    def drain(r, carry):
        _row_copy(y_hbm, 0, buf_ref.at[0], r, sem).wait()
        _row_copy(y_hbm, 0, buf_ref.at[1], r, sem).wait()
        return carry

    lax.fori_loop(0, tm, drain, 0, unroll=ROW_DMA_UNROLL)
    it = it_ref[...]
    f = buf_ref[0] * it[:, 2:3] + buf_ref[1] * it[:, 3:4]
    o_ref[...] = _layer_norm(alpha * x_ref[...] + f, g_ref[...], b_ref[...])


def _moe_combine(dest, y_sorted, x1, info_t, ln_g, ln_b, alpha):
    nt, d = x1.shape
    kern = functools.partial(_moe_combine_kernel, alpha=alpha, nt=nt)
    return pl.pallas_call(
        kern,
        grid_spec=pltpu.PrefetchScalarGridSpec(
            num_scalar_prefetch=1,
            grid=(nt // ROW_TILE,),
            in_specs=[pl.BlockSpec(memory_space=pl.ANY),
                      pl.BlockSpec((ROW_TILE, d), lambda i, ds: (i, 0)),
                      pl.BlockSpec((ROW_TILE, HEAD_DIM), lambda i, ds: (i, 0)),
                      pl.BlockSpec((1, d), lambda i, ds: (0, 0)),
                      pl.BlockSpec((1, d), lambda i, ds: (0, 0))],
            out_specs=pl.BlockSpec((ROW_TILE, d), lambda i, ds: (i, 0)),
            scratch_shapes=[pltpu.VMEM((2, ROW_TILE, d), F32), pltpu.SemaphoreType.DMA(())]),
        out_shape=jax.ShapeDtypeStruct((nt, d), F32),
        compiler_params=_params(("arbitrary",)),
        name="moe_combine",
    )(dest, y_sorted, x1, info_t, ln_g, ln_b)


def _moe_layer(y_mix, x, w_out_bf16, ln1_g, ln1_b, wr_t, rb_col, wg, wu, wd, first_expert, ln2_g, ln2_b,
               alpha, n_valid):
    nt, d = x.shape
    x1, info, info_t, cnt = _outproj_router(y_mix, x, w_out_bf16, ln1_g, ln1_b, wr_t, rb_col, alpha, n_valid)
    n_blocks = -(-(2 * n_valid) // MOE_TILE) + N_EXPERTS
    counts = cnt[:, 0].astype(I32)
    padded = (counts + MOE_TILE - 1) // MOE_TILE * MOE_TILE
    pad_end = jnp.cumsum(padded)
    pad_start = pad_end - padded
    e1, e2 = info[0].astype(I32), info[1].astype(I32)
    dest1 = pad_start[e1] + info[4].astype(I32)
    dest2 = pad_start[e2] + info[5].astype(I32)
    tok = jnp.arange(n_valid, dtype=I32)
    row_tok = jnp.zeros((n_blocks * MOE_TILE,), I32).at[dest1[:n_valid]].set(tok).at[dest2[:n_valid]].set(tok)
    n_used = (pad_end[-1] // MOE_TILE).astype(I32)
    blk_start = jnp.arange(n_blocks, dtype=I32) * MOE_TILE
    blk_e = jnp.minimum(jnp.sum(pad_end[None, :] <= blk_start[:, None], axis=1), N_EXPERTS - 1).astype(I32)
    blk_e = jnp.where(jnp.arange(n_blocks) < n_used, blk_e, blk_e[jnp.maximum(n_used - 1, 0)])
    valid = jnp.arange(nt) < n_valid
    dest = jnp.concatenate([jnp.where(valid, dest1, 0), jnp.where(valid, dest2, 0)]).astype(I32)
    nu = n_used.reshape(1)
    xs = _moe_gather(row_tok, nu, x1)
    ys = _moe_experts(blk_e + first_expert, nu, xs, wg, wu, wd)
    return _moe_combine(dest, ys, x1, info_t, ln2_g, ln2_b, alpha)


def kernel(x_prompt, x_sample, cache_k, cache_v, state_dn, state_conv, page_table, ln_in_g, ln_in_b, rel_bias,
           w_router, router_bias, w_in, w_out, conv_w, a_log, dt_bias, dn_norm_w, ln1_g, ln1_b, ln2_g, ln2_b,
           w_gate, w_up, w_down):
    batch, seq, d = x_prompt.shape
    n_seq, dec_seq, _ = x_sample.shape
    depth = w_in.shape[0]
    n_pool, page = cache_k.shape[1], cache_k.shape[2]
    kvh = cache_k.shape[3]
    n_pages = page_table.shape[1]
    past_len = n_pages * page
    assert dec_seq == 1 and past_len % MOBA_BLOCK == 0 and n_seq <= ROW_TILE
    att_w = d // 2
    dn_w = d - att_w
    heads = att_w // HEAD_DIM
    dn_heads = dn_w // HEAD_DIM
    kv_w = kvh * HEAD_DIM
    assert dn_heads == 8 and heads == ATT_GROUP * kvh
    alpha = (2 * depth) ** 0.25
    n_prompt = batch * seq
    nt = n_prompt + ROW_TILE
    n_valid = n_prompt + n_seq

    x = jnp.concatenate([x_prompt.reshape(n_prompt, d), x_sample.reshape(n_seq, d),
                         jnp.zeros((ROW_TILE - n_seq, d), F32)], axis=0)
    x = _input_layer_norm(x, ln_in_g, ln_in_b)

    widths = (att_w, kv_w, kv_w, 3 * dn_w, dn_w, HEAD_DIM)
    in_cols = w_in.shape[2]
    w_in_b = jnp.pad(w_in, ((0, 0), (0, 0), (0, sum(widths) - in_cols))).astype(BF16)
    w_out_b = w_out.astype(BF16)
    all_experts = lambda w: w.astype(BF16).reshape((depth * N_EXPERTS,) + w.shape[2:])
    wg_b, wu_b, wd_b = all_experts(w_gate), all_experts(w_up), all_experts(w_down)
    wr_t = w_router.T.astype(BF16)
    rb_col = jnp.broadcast_to(router_bias.astype(F32)[:, None], (N_EXPERTS, HEAD_DIM))
    lane_pad = lambda v: jnp.pad(v.astype(F32), ((0, 0), (dn_heads, HEAD_DIM - 2 * dn_heads)))
    al_rows, dtb_rows = lane_pad(a_log), lane_pad(dt_bias)
    bias_tiles = _bias_tiles(rel_bias, seq // MOBA_BLOCK)
    pt = page_table.astype(I32)

    ks_p, vs_p, dn_p, cv_p, ks_s, vs_s, dn_s, cv_s = [], [], [], [], [], [], [], []
    for l in range(depth):
        qa, ka, va, qkvb, og, bg = _in_projection(x, w_in_b[l], widths)
        al_row, dtb_row = al_rows[l:l + 1], dtb_rows[l:l + 1]
        norm_w = dn_norm_w[l].reshape(1, HEAD_DIM).astype(F32)

        ya_p = _moba_prompt(qa, ka, va, bias_tiles, batch, seq)
        qn, kn, vv, gb = _dn_prep(qkvb, bg, conv_w[l], al_row, dtb_row, batch, seq)
        yb_p, s_p = _dn_chunks(qn, kn, vv, gb, og, norm_w, batch, seq)

        sl = slice(n_prompt, n_prompt + n_seq)
        q3 = qa[sl].reshape(n_seq, heads, HEAD_DIM)
        kn3 = ka[sl].reshape(n_seq, kvh, HEAD_DIM)
        vn3 = va[sl].reshape(n_seq, kvh, HEAD_DIM)
        ksum = _past_block_sums(cache_k, l, pt)
        top = _decode_select(q3, ksum.transpose(0, 2, 1, 3))[:, :, :MOBA_TOPK]
        ya_s = _decode_attend(q3, kn3, vn3, cache_k, cache_v, l, pt, top, rel_bias, past_len)
        yb_s, s_s, c_s = _dn_step(qkvb[sl].reshape(n_seq, 1, 3 * dn_w), state_conv[l],
                                  bg[sl].reshape(n_seq, 1, HEAD_DIM), og[sl].reshape(n_seq, 1, dn_w),
                                  state_dn[l], conv_w[l], al_row, dtb_row, norm_w)

        y_s = jnp.concatenate([ya_s.reshape(n_seq, att_w), yb_s.reshape(n_seq, dn_w)], axis=1).astype(BF16)
        y_mix = jnp.concatenate([jnp.concatenate([ya_p, yb_p], axis=1), y_s,
                                 jnp.zeros((ROW_TILE - n_seq, d), BF16)], axis=0)
        x = _moe_layer(y_mix, x, w_out_b[l], ln1_g[l].reshape(1, d), ln1_b[l].reshape(1, d), wr_t, rb_col,
                       wg_b, wu_b, wd_b, l * N_EXPERTS, ln2_g[l].reshape(1, d), ln2_b[l].reshape(1, d),
                       alpha, n_valid)

        ks_p.append(ka[:n_prompt].reshape(batch, seq, kvh, HEAD_DIM))
        vs_p.append(va[:n_prompt].reshape(batch, seq, kvh, HEAD_DIM))
        dn_p.append(s_p)
        cv_p.append(qkvb[:n_prompt].reshape(batch, seq, 3 * dn_w)[:, seq - (CONV_W - 1):])
        ks_s.append(kn3.reshape(n_seq, 1, kvh, HEAD_DIM))
        vs_s.append(vn3.reshape(n_seq, 1, kvh, HEAD_DIM))
        dn_s.append(s_s)
        cv_s.append(c_s)

    y_prompt = x[:n_prompt].reshape(batch, seq, d)
    y_sample = x[n_prompt:n_prompt + n_seq].reshape(n_seq, 1, d)
    return (y_prompt, y_sample, jnp.stack(ks_p), jnp.stack(vs_p), jnp.stack(dn_p), jnp.stack(cv_p),
            jnp.stack(ks_s), jnp.stack(vs_s), jnp.stack(dn_s), jnp.stack(cv_s))
```

```python
import functools
import math

import jax
import jax.numpy as jnp
from jax import lax
from jax.experimental import pallas as pl
from jax.experimental.pallas import tpu as pltpu

F32 = jnp.float32
BF16 = jnp.bfloat16
I32 = jnp.int32
HIGHEST = lax.Precision.HIGHEST

HEAD_DIM = 128
ATT_GROUP = 2
MOBA_BLOCK = 256
MOBA_TOPK = 3
MOBA_Q_SUB = 256
REL_BUCKETS = 32
REL_MAX_DIST = 2048
DN_CHUNK = 64
CONV_W = 4
N_EXPERTS = 16
EXPERTS_PER_GROUP = 4
LN_EPS = 1e-5
NORM_EPS = 1e-6
ROW_TILE = 256
MOE_TILE = 256
ROW_DMA_UNROLL = 8
PAGES_PER_STEP = 16
NEG = -1e30
VMEM_LIMIT_V7X = 56 * 1024 * 1024


def _params(semantics):
    return pltpu.CompilerParams(dimension_semantics=semantics, vmem_limit_bytes=VMEM_LIMIT_V7X)


def _nt_dot(a, b, precision=None):
    return lax.dot_general(a, b, (((1,), (1,)), ((), ())), precision=precision, preferred_element_type=F32)


def _tn_dot(a, b, precision=None):
    return lax.dot_general(a, b, (((0,), (0,)), ((), ())), precision=precision, preferred_element_type=F32)


def _hdot(a, b):
    return jnp.dot(a, b, precision=HIGHEST, preferred_element_type=F32)


def _bdot(a, b):
    return jnp.dot(a.astype(BF16), b.astype(BF16), preferred_element_type=F32)


def _dot3(a, b):
    ah, bh = a.astype(BF16), b.astype(BF16)
    al, bl = (a - ah.astype(F32)).astype(BF16), (b - bh.astype(F32)).astype(BF16)
    dot = functools.partial(jnp.dot, preferred_element_type=F32)
    return dot(ah, bh) + (dot(ah, bl) + dot(al, bh))


def _layer_norm(x, g, b):
    mu = jnp.mean(x, axis=-1, keepdims=True)
    xc = x - mu
    var = jnp.mean(xc * xc, axis=-1, keepdims=True)
    return xc * lax.rsqrt(var + LN_EPS) * g + b


def _silu(x):
    return x * (1.0 / (1.0 + jnp.exp(-x)))


def _sigmoid(x):
    return 1.0 / (1.0 + jnp.exp(-x))


def _softplus(x):
    return jnp.maximum(x, 0.0) + jnp.log(1.0 + jnp.exp(-jnp.abs(x)))


def _rel_bucket(dist):
    n = jnp.maximum(dist, 0)
    max_exact = REL_BUCKETS // 2
    nf = jnp.maximum(n, 1).astype(F32)
    large = max_exact + (jnp.log(nf / max_exact) / math.log(REL_MAX_DIST / max_exact)
                         * (REL_BUCKETS - max_exact)).astype(I32)
    large = jnp.minimum(large, REL_BUCKETS - 1)
    return jnp.where(n < max_exact, n, large)


def _ln_kernel(x_ref, g_ref, b_ref, o_ref):
    o_ref[...] = _layer_norm(x_ref[...], g_ref[...], b_ref[...])


def _input_layer_norm(x, g, b):
    nt, d = x.shape
    return pl.pallas_call(
        _ln_kernel,
        grid=(nt // ROW_TILE,),
        in_specs=[pl.BlockSpec((ROW_TILE, d), lambda i: (i, 0)),
                  pl.BlockSpec((1, d), lambda i: (0, 0)),
                  pl.BlockSpec((1, d), lambda i: (0, 0))],
        out_specs=pl.BlockSpec((ROW_TILE, d), lambda i: (i, 0)),
        out_shape=jax.ShapeDtypeStruct((nt, d), F32),
        compiler_params=_params(("parallel",)),
        name="input_layer_norm",
    )(x, g.reshape(1, d), b.reshape(1, d))


def _inproj_kernel(x_ref, w_ref, *o_refs):
    xb = x_ref[...].astype(BF16)
    off = 0
    for o_ref in o_refs:
        n = o_ref.shape[1]
        o_ref[...] = jnp.dot(xb, w_ref[:, off:off + n], preferred_element_type=F32)
        off += n


def _in_projection(x, w_bf16, widths):
    nt, d = x.shape
    cols = w_bf16.shape[1]
    assert sum(widths) == cols
    return pl.pallas_call(
        _inproj_kernel,
        grid=(nt // ROW_TILE,),
        in_specs=[pl.BlockSpec((ROW_TILE, d), lambda i: (i, 0)),
                  pl.BlockSpec((d, cols), lambda i: (0, 0), pipeline_mode=pl.Buffered(1))],
        out_specs=[pl.BlockSpec((ROW_TILE, n), lambda i: (i, 0)) for n in widths],
        out_shape=[jax.ShapeDtypeStruct((nt, n), F32) for n in widths],
        compiler_params=_params(("parallel",)),
        name="in_projection",
    )(x, w_bf16)


def _bias_tile_kernel(tab_ref, o_ref):
    h = pl.program_id(0)
    d = pl.program_id(1)
    key = lax.broadcasted_iota(I32, (MOBA_BLOCK, MOBA_BLOCK), 0)
    qry = lax.broadcasted_iota(I32, (MOBA_BLOCK, MOBA_BLOCK), 1)
    dist = d * MOBA_BLOCK + qry - key
    bucket = _rel_bucket(dist)
    acc = jnp.zeros((MOBA_BLOCK, MOBA_BLOCK), F32)
    for k in range(REL_BUCKETS):
        acc = jnp.where(bucket == k, tab_ref[k, h], acc)
    o_ref[0, 0] = jnp.where(dist >= 0, acc, NEG)


def _bias_tiles(rel_bias, n_blk):
    heads = rel_bias.shape[1]
    return pl.pallas_call(
        _bias_tile_kernel,
        grid=(heads, n_blk),
        in_specs=[pl.BlockSpec(memory_space=pltpu.SMEM)],
        out_specs=pl.BlockSpec((1, 1, MOBA_BLOCK, MOBA_BLOCK), lambda h, d: (h, d, 0, 0)),
        out_shape=jax.ShapeDtypeStruct((heads, n_blk, MOBA_BLOCK, MOBA_BLOCK), F32),
        compiler_params=_params(("parallel", "parallel")),
        name="bias_tiles",
    )(rel_bias.astype(F32))


def _moba_prompt_kernel(q_ref, k_ref, v_ref, bias_ref, o_ref, kaug_ref, vt_ref, kmean_ref):
    i = pl.program_id(2)
    seq = k_ref.shape[0]
    n_blk = seq // MOBA_BLOCK
    scale = HEAD_DIM ** -0.5

    @pl.when(i == 0)
    def _():
        kaug_ref[:, :HEAD_DIM] = k_ref[...].astype(BF16)
        rows = lax.broadcasted_iota(I32, (seq, HEAD_DIM), 0) // MOBA_BLOCK
        lanes = lax.broadcasted_iota(I32, (seq, HEAD_DIM), 1)
        kaug_ref[:, HEAD_DIM:] = jnp.where(rows == lanes, 1.0, 0.0).astype(BF16)
        for j in range(n_blk):
            blk = slice(j * MOBA_BLOCK, (j + 1) * MOBA_BLOCK)
            vt_ref[j] = v_ref[blk, :].T.astype(BF16)
            kmean_ref[j:j + 1, :] = jnp.mean(k_ref[blk, :], axis=0, keepdims=True)

    tq = q_ref.shape[0]
    kmean = kmean_ref[...].astype(BF16)
    chains = []
    for hh in range(ATT_GROUP):
        q = q_ref[:, hh * HEAD_DIM:(hh + 1) * HEAD_DIM]
        gate = _nt_dot(kmean, q.astype(BF16))
        jidx = lax.broadcasted_iota(I32, gate.shape, 0)
        rank = jnp.zeros(gate.shape, I32)
        for jp in range(n_blk):
            row = gate[jp:jp + 1, :]
            before = (row > gate) | ((row == gate) & (jp < jidx))
            rank = rank + jnp.where(before, jnp.where(jp < i, 1, 0), 0)
        sel = ((jidx < i) & (rank < MOBA_TOPK)) | (jidx == i)
        selneg = jnp.where(sel, 0.0, NEG)
        selneg = jnp.concatenate([selneg, jnp.zeros((HEAD_DIM - n_blk, tq), F32)], axis=0)
        q_aug_t = jnp.concatenate([q.T.astype(BF16), selneg.astype(BF16)], axis=0)
        for c0 in range(0, tq, MOBA_Q_SUB):
            chains.append((hh, c0, q_aug_t[:, c0:c0 + MOBA_Q_SUB]))

    dot = functools.partial(jnp.dot, preferred_element_type=F32)

    def logits(t):
        j = jnp.maximum(i - t, 0)
        tb = jnp.minimum(t, n_blk - 1)
        kj = kaug_ref[pl.ds(pl.multiple_of(j * MOBA_BLOCK, MOBA_BLOCK), MOBA_BLOCK), :]
        return tuple(dot(kj, qc) * scale + bias_ref[hh, tb, :, c0:c0 + MOBA_Q_SUB] for hh, c0, qc in chains)

    def body(t, carry):
        ss, stats = carry
        ss_next = logits(t + 1)
        vtj = vt_ref[i - t]
        m_new = [jnp.maximum(c[0], jnp.max(s, axis=0, keepdims=True)) for c, s in zip(stats, ss)]
        ps = [jnp.exp(s - mn) for s, mn in zip(ss, m_new)]
        alphas = [jnp.exp(c[0] - mn) for c, mn in zip(stats, m_new)]
        pvs = [dot(vtj, p.astype(BF16)) for p in ps]
        ls = [a * c[1] + jnp.sum(p, axis=0, keepdims=True) for a, c, p in zip(alphas, stats, ps)]
        accs = [a * c[2] + pv for a, c, pv in zip(alphas, stats, pvs)]
        return ss_next, tuple(zip(m_new, ls, accs))

    init = tuple((jnp.full((1, MOBA_Q_SUB), -jnp.inf, F32), jnp.zeros((1, MOBA_Q_SUB), F32),
                  jnp.zeros((HEAD_DIM, MOBA_Q_SUB), F32)) for _ in chains)
    _, final = lax.fori_loop(0, i + 1, body, (logits(0), init))
    for (hh, c0, _), (_, l, acc) in zip(chains, final):
        o_ref[c0:c0 + MOBA_Q_SUB, hh * HEAD_DIM:(hh + 1) * HEAD_DIM] = (acc / l).T.astype(o_ref.dtype)


def _moba_prompt(qa, ka, va, bias_tiles, batch, seq):
    kvh = ka.shape[1] // HEAD_DIM
    group_w = ATT_GROUP * HEAD_DIM
    n_blk = seq // MOBA_BLOCK
    assert seq % MOBA_BLOCK == 0 and n_blk <= HEAD_DIM
    return pl.pallas_call(
        _moba_prompt_kernel,
        grid=(kvh, batch, n_blk),
        in_specs=[pl.BlockSpec((MOBA_BLOCK, group_w), lambda g, b, i: (b * n_blk + i, g)),
                  pl.BlockSpec((seq, HEAD_DIM), lambda g, b, i: (b, g)),
                  pl.BlockSpec((seq, HEAD_DIM), lambda g, b, i: (b, g)),
                  pl.BlockSpec((ATT_GROUP, n_blk, MOBA_BLOCK, MOBA_BLOCK), lambda g, b, i: (g, 0, 0, 0))],
        out_specs=pl.BlockSpec((MOBA_BLOCK, group_w), lambda g, b, i: (b * n_blk + i, g)),
        out_shape=jax.ShapeDtypeStruct((batch * seq, kvh * group_w), BF16),
        scratch_shapes=[pltpu.VMEM((seq, 2 * HEAD_DIM), BF16),
                        pltpu.VMEM((n_blk, HEAD_DIM, MOBA_BLOCK), BF16),
                        pltpu.VMEM((n_blk, HEAD_DIM), F32)],
        compiler_params=_params(("parallel", "parallel", "arbitrary")),
        name="moba_prompt",
    )(qa, ka, va, bias_tiles)


def _page_sum_kernel(pt_ref, *refs):
    o_ref = refs[-1]
    for u in range(0, PAGES_PER_STEP, 2):
        o_ref[0, u // 2] = jnp.sum(refs[u][0, 0], axis=0) + jnp.sum(refs[u + 1][0, 0], axis=0)


def _past_block_sums(cache, layer, page_table):
    n_seq, n_pages = page_table.shape
    page, kvh = cache.shape[2], cache.shape[3]
    assert 2 * page == MOBA_BLOCK and n_pages % PAGES_PER_STEP == 0
    steps = n_pages // PAGES_PER_STEP
    blk_per_step = PAGES_PER_STEP // 2

    def page_spec(u):
        return pl.BlockSpec((1, 1, page, kvh, HEAD_DIM),
                            lambda s, t, pt: (layer, pt[s, t * PAGES_PER_STEP + u], 0, 0, 0))

    return pl.pallas_call(
        _page_sum_kernel,
        grid_spec=pltpu.PrefetchScalarGridSpec(
            num_scalar_prefetch=1,
            grid=(n_seq, steps),
            in_specs=[page_spec(u) for u in range(PAGES_PER_STEP)],
            out_specs=pl.BlockSpec((1, blk_per_step, kvh, HEAD_DIM), lambda s, t, pt: (s, t, 0, 0))),
        out_shape=jax.ShapeDtypeStruct((n_seq, n_pages // 2, kvh, HEAD_DIM), F32),
        compiler_params=_params(("parallel", "arbitrary")),
        name="past_block_sums",
    )(page_table, *([cache] * PAGES_PER_STEP))


def _decode_select_kernel(q_ref, ksum_ref, o_ref):
    heads = q_ref.shape[1]
    n_blk = ksum_ref.shape[2]
    q = q_ref[0]
    qb = jnp.concatenate([q, q], axis=0).astype(BF16)
    hrow = lax.broadcasted_iota(I32, (heads, n_blk), 0)
    gate = jnp.zeros((heads, n_blk), F32)
    for g in range(heads // ATT_GROUP):
        kmean = ksum_ref[0, g] * (1.0 / MOBA_BLOCK)
        gate = jnp.where(hrow // ATT_GROUP == g, _nt_dot(qb, kmean.astype(BF16))[:heads], gate)
    nidx = lax.broadcasted_iota(I32, (heads, n_blk), 1)
    rank = jnp.zeros((heads, n_blk), I32)
    for n in range(n_blk):
        col = gate[:, n:n + 1]
        rank = rank + jnp.where((col > gate) | ((col == gate) & (n < nidx)), 1, 0)
    lane = lax.broadcasted_iota(I32, (heads, HEAD_DIM), 1)
    out = jnp.zeros((heads, HEAD_DIM), I32)
    for r in range(MOBA_TOPK):
        idx = jnp.sum(jnp.where(rank == r, nidx.astype(F32), 0.0), axis=1, keepdims=True)
        out = jnp.where(lane == r, idx.astype(I32), out)
    o_ref[0] = out


def _decode_select(q3, ksum):
    n_seq, heads, _ = q3.shape
    kvh, n_blk = ksum.shape[1], ksum.shape[2]
    return pl.pallas_call(
        _decode_select_kernel,
        grid=(n_seq,),
        in_specs=[pl.BlockSpec((1, heads, HEAD_DIM), lambda s: (s, 0, 0)),
                  pl.BlockSpec((1, kvh, n_blk, HEAD_DIM), lambda s: (s, 0, 0, 0))],
        out_specs=pl.BlockSpec((1, heads, HEAD_DIM), lambda s: (s, 0, 0)),
        out_shape=jax.ShapeDtypeStruct((n_seq, heads, HEAD_DIM), I32),
        compiler_params=_params(("parallel",)),
        name="decode_select",
    )(q3, ksum)


def _decode_attend_kernel(pt_ref, blk_ref, q_ref, kn_ref, vn_ref, tab_ref, ck_hbm, cv_hbm, o_ref,
                          kbuf, vbuf, sems, *, past_len, layer):
    n_tiles = 2 * MOBA_TOPK
    s = pl.program_id(0)
    heads = q_ref.shape[1]
    page = kbuf.shape[2]
    scale = HEAD_DIM ** -0.5
    rows = 16
    bround = lambda a: a.astype(BF16).astype(F32)
    lane = lax.broadcasted_iota(I32, (1, page), 1)

    def tile_copies(h):
        g = h // ATT_GROUP
        copies = []
        for u in range(n_tiles):
            pg = pt_ref[s, 2 * blk_ref[s * heads + h, u // 2] + (u % 2)]
            copies.append(pltpu.make_async_copy(ck_hbm.at[layer, pg, :, g, :], kbuf.at[h, u], sems.at[h]))
            copies.append(pltpu.make_async_copy(cv_hbm.at[layer, pg, :, g, :], vbuf.at[h, u], sems.at[h]))
        return copies

    for h in range(heads):
        for cp in tile_copies(h):
            cp.start()

    for h in range(heads):
        for cp in tile_copies(h):
            cp.wait()
        q = jnp.broadcast_to(q_ref[0, h:h + 1, :], (rows, HEAD_DIM)).astype(BF16)
        logits = []
        for u in range(n_tiles):
            kpos = blk_ref[s * heads + h, u // 2] * MOBA_BLOCK + (u % 2) * page + lane
            bucket = _rel_bucket(past_len - kpos)
            bias = jnp.zeros((1, page), F32)
            for k in range(REL_BUCKETS):
                bias = jnp.where(bucket == k, tab_ref[k, h], bias)
            logits.append(_nt_dot(q, kbuf[h, u].astype(BF16))[0:1, :] * scale + bias)
        g = h // ATT_GROUP
        kn = kn_ref[0, g:g + 1, :]
        vn = vn_ref[0, g:g + 1, :]
        own = jnp.sum(q[0:1, :].astype(F32) * bround(kn), axis=1, keepdims=True) * scale + tab_ref[0, h]

        m = own
        for lg in logits:
            m = jnp.maximum(m, jnp.max(lg, axis=1, keepdims=True))
        e_own = jnp.exp(own - m)
        es = [jnp.exp(lg - m) for lg in logits]
        den = e_own
        for e in es:
            den = den + jnp.sum(e, axis=1, keepdims=True)
        acc = bround(e_own / den) * bround(vn)
        for u in range(n_tiles):
            p = jnp.broadcast_to(es[u] / den, (rows, page)).astype(BF16)
            acc = acc + jnp.dot(p, vbuf[h, u].astype(BF16), preferred_element_type=F32)[0:1, :]
        o_ref[0, h:h + 1, :] = acc


def _decode_attend(q3, kn3, vn3, cache_k, cache_v, layer, page_table, blocks, rel_bias, past_len):
    n_seq, heads, _ = q3.shape
    kvh = kn3.shape[1]
    page = cache_k.shape[2]
    n_tiles = 2 * MOBA_TOPK
    return pl.pallas_call(
        functools.partial(_decode_attend_kernel, past_len=past_len, layer=layer),
        grid_spec=pltpu.PrefetchScalarGridSpec(
            num_scalar_prefetch=2,
            grid=(n_seq,),
            in_specs=[pl.BlockSpec((1, heads, HEAD_DIM), lambda s, pt, bk: (s, 0, 0)),
                      pl.BlockSpec((1, kvh, HEAD_DIM), lambda s, pt, bk: (s, 0, 0)),
                      pl.BlockSpec((1, kvh, HEAD_DIM), lambda s, pt, bk: (s, 0, 0)),
                      pl.BlockSpec(memory_space=pltpu.SMEM),
                      pl.BlockSpec(memory_space=pl.ANY),
                      pl.BlockSpec(memory_space=pl.ANY)],
            out_specs=pl.BlockSpec((1, heads, HEAD_DIM), lambda s, pt, bk: (s, 0, 0)),
            scratch_shapes=[pltpu.VMEM((heads, n_tiles, page, HEAD_DIM), F32),
                            pltpu.VMEM((heads, n_tiles, page, HEAD_DIM), F32),
                            pltpu.SemaphoreType.DMA((heads,))]),
        out_shape=jax.ShapeDtypeStruct((n_seq, heads, HEAD_DIM), F32),
        compiler_params=_params(("arbitrary",)),
        name="decode_attend",
    )(page_table, blocks.reshape(n_seq * heads, MOBA_TOPK), q3, kn3, vn3, rel_bias.astype(F32), cache_k, cache_v)


def _conv_gate_math(bg, al, dtb):
    lane = lax.broadcasted_iota(I32, bg.shape, 1)
    return jnp.where(lane < 8, _sigmoid(bg), -jnp.exp(al) * _softplus(bg + dtb))


def _head_l2norm(x, n_heads, mult):
    outs = []
    for h in range(n_heads):
        seg = x[:, h * HEAD_DIM:(h + 1) * HEAD_DIM]
        ss = jnp.sum(seg * seg, axis=1, keepdims=True)
        outs.append(seg * (lax.rsqrt(ss + NORM_EPS) * mult))
    return jnp.concatenate(outs, axis=1)


def _dn_prep_kernel(cur_ref, prev_ref, bg_ref, cw_ref, al_ref, dtb_ref, qn_ref, kn_ref, vv_ref, gb_ref):
    i = pl.program_id(1)
    t, width3 = cur_ref.shape
    width = width3 // 3
    n_heads = width // HEAD_DIM
    cur = cur_ref[...]
    prev = prev_ref[...] * jnp.where(i > 0, 1.0, 0.0)
    row8 = lax.broadcasted_iota(I32, (8, 1), 0)
    y = cur * cw_ref[CONV_W - 1:CONV_W, :]
    for k in range(1, CONV_W):
        rolled = pltpu.roll(cur, k, 0)
        head = jnp.where(row8 < k, pltpu.roll(prev, k, 0), rolled[:8])
        shifted = jnp.concatenate([head, rolled[8:]], axis=0)
        y = y + shifted * cw_ref[CONV_W - 1 - k:CONV_W - k, :]
    a = _silu(y)
    qn_ref[...] = _head_l2norm(a[:, :width], n_heads, HEAD_DIM ** -0.5)
    kn_ref[...] = _head_l2norm(a[:, width:2 * width], n_heads, 1.0)
    vv_ref[...] = a[:, 2 * width:]
    gb_ref[...] = _conv_gate_math(bg_ref[...], al_ref[...], dtb_ref[...])


def _dn_prep(qkvb, bg, conv_w, al_row, dtb_row, batch, seq):
    width3 = qkvb.shape[1]
    width = width3 // 3
    tiles = seq // ROW_TILE
    rows = batch * seq
    row_spec = lambda n: pl.BlockSpec((ROW_TILE, n), lambda b, i: (b * tiles + i, 0))
    return pl.pallas_call(
        _dn_prep_kernel,
        grid=(batch, tiles),
        in_specs=[row_spec(width3),
                  pl.BlockSpec((8, width3), lambda b, i: (jnp.maximum((b * tiles + i) * (ROW_TILE // 8) - 1, 0), 0)),
                  row_spec(HEAD_DIM),
                  pl.BlockSpec((CONV_W, width3), lambda b, i: (0, 0)),
                  pl.BlockSpec((1, HEAD_DIM), lambda b, i: (0, 0)),
                  pl.BlockSpec((1, HEAD_DIM), lambda b, i: (0, 0))],
        out_specs=[row_spec(width), row_spec(width), row_spec(width), row_spec(HEAD_DIM)],
        out_shape=[jax.ShapeDtypeStruct((rows, width), F32)] * 3 + [jax.ShapeDtypeStruct((rows, HEAD_DIM), F32)],
        compiler_params=_params(("parallel", "parallel")),
        name="dn_prep",
    )(qkvb, qkvb, bg, conv_w, al_row, dtb_row)


def _dn_chunk_kernel(qn_ref, kn_ref, vv_ref, gb_ref, og_ref, nw_ref, y_ref, s_out_ref, s_ref):
    c = pl.program_id(1)
    n_chunks = pl.num_programs(1)
    ch, width = qn_ref.shape
    n_heads = width // HEAD_DIM

    @pl.when(c == 0)
    def _():
        s_ref[...] = jnp.zeros(s_ref.shape, F32)

    ri = lax.broadcasted_iota(I32, (ch, ch), 0)
    ci = lax.broadcasted_iota(I32, (ch, ch), 1)
    tri = ri >= ci
    strict = ri > ci
    eye = ri == ci
    gbv = gb_ref[...]
    gcum_all = _hdot(jnp.where(tri, 1.0, 0.0), gbv)

    hs = range(n_heads)
    sls = [slice(h * HEAD_DIM, (h + 1) * HEAD_DIM) for h in hs]
    dot = functools.partial(jnp.dot, preferred_element_type=F32)
    q = [qn_ref[:, sl] for sl in sls]
    k = [kn_ref[:, sl] for sl in sls]
    beta = [gbv[:, h:h + 1] for h in hs]
    gc = [gcum_all[:, 8 + h:9 + h] for h in hs]
    gcb = [jnp.broadcast_to(g, (ch, ch)) for g in gc]
    grow = [jnp.sum(jnp.where(eye, g, 0.0), axis=0, keepdims=True) for g in gcb]
    decay = [jnp.where(tri, jnp.exp(jnp.where(tri, g - r, 0.0)), 0.0) for g, r in zip(gcb, grow)]
    kbeta = [kk * b for kk, b in zip(k, beta)]
    egc = [jnp.exp(g) for g in gc]
    k16 = [kk.astype(BF16) for kk in k]
    a = [jnp.where(strict, _nt_dot(kb.astype(BF16), kk) * d, 0.0) for kb, kk, d in zip(kbeta, k16, decay)]
    x = [jnp.concatenate([vv_ref[:, sl] * b, kb * e], axis=1) for sl, b, kb, e in zip(sls, beta, kbeta, egc)]
    npow = [-aa for aa in a]
    span = 1
    while True:
        mm = _dot3 if span <= 2 else _bdot
        x = [xx + mm(n, xx) for n, xx in zip(npow, x)]
        span *= 2
        if span >= ch:
            break
        npow = [mm(n, n) for n in npow]
    s = [s_ref[h] for h in hs]
    s16 = [ss.astype(BF16) for ss in s]
    v_new = [xx[:, :HEAD_DIM] - dot(xx[:, HEAD_DIM:].astype(BF16), ss) for xx, ss in zip(x, s16)]
    v16 = [vn.astype(BF16) for vn in v_new]
    attn = [jnp.where(tri, _nt_dot(qq.astype(BF16), kk) * d, 0.0) for qq, kk, d in zip(q, k16, decay)]
    o = [dot((qq * e).astype(BF16), ss) + dot(at.astype(BF16), vn)
         for qq, e, ss, at, vn in zip(q, egc, s16, attn, v16)]
    g_last = [g[ch - 1:ch, :] for g in gc]
    kdec = [(kk * jnp.exp(gl - g)).astype(BF16) for kk, gl, g in zip(k, g_last, gc)]
    s_new = [ss * jnp.exp(gl) + _tn_dot(kd, vn) for ss, gl, kd, vn in zip(s, g_last, kdec, v16)]
    for h in hs:
        s_ref[h] = s_new[h]
    for h, sl in enumerate(sls):
        on = o[h] * lax.rsqrt(jnp.mean(o[h] * o[h], axis=1, keepdims=True) + NORM_EPS) * nw_ref[...]
        y_ref[:, sl] = (on * _silu(og_ref[:, sl])).astype(y_ref.dtype)

    @pl.when(c == n_chunks - 1)
    def _():
        s_out_ref[0] = s_ref[...]


def _dn_chunks(qn, kn, vv, gb, og, norm_w, batch, seq):
    width = qn.shape[1]
    n_heads = width // HEAD_DIM
    assert seq % DN_CHUNK == 0
    n_chunks = seq // DN_CHUNK
    row_spec = lambda n: pl.BlockSpec((DN_CHUNK, n), lambda b, c: (b * n_chunks + c, 0))
    return pl.pallas_call(
        _dn_chunk_kernel,
        grid=(batch, n_chunks),
        in_specs=[row_spec(width), row_spec(width), row_spec(width), row_spec(HEAD_DIM), row_spec(width),
                  pl.BlockSpec((1, HEAD_DIM), lambda b, c: (0, 0))],
        out_specs=[row_spec(width),
                   pl.BlockSpec((1, n_heads, HEAD_DIM, HEAD_DIM), lambda b, c: (b, 0, 0, 0))],
        out_shape=[jax.ShapeDtypeStruct((batch * seq, width), BF16),
                   jax.ShapeDtypeStruct((batch, n_heads, HEAD_DIM, HEAD_DIM), F32)],
        scratch_shapes=[pltpu.VMEM((n_heads, HEAD_DIM, HEAD_DIM), F32)],
        compiler_params=_params(("parallel", "arbitrary")),
        name="dn_chunks",
    )(qn, kn, vv, gb, og, norm_w)


def _dn_step_kernel(x_ref, cs_ref, bg_ref, og_ref, s_ref, cw_ref, al_ref, dtb_ref, nw_ref,
                    y_ref, s_out_ref, cs_out_ref):
    width3 = x_ref.shape[2]
    width = width3 // 3
    n_heads = width // HEAD_DIM
    xn = x_ref[0]
    cs = cs_ref[0]
    y = xn * cw_ref[CONV_W - 1:CONV_W, :]
    for j in range(CONV_W - 1):
        y = y + cs[j:j + 1, :] * cw_ref[j:j + 1, :]
    for j in range(1, CONV_W - 1):
        cs_out_ref[0, j - 1:j, :] = cs_ref[0, j:j + 1, :]
    cs_out_ref[0, CONV_W - 2:CONV_W - 1, :] = xn
    a = _silu(y)
    qn = _head_l2norm(a[:, :width], n_heads, HEAD_DIM ** -0.5)
    kn = _head_l2norm(a[:, width:2 * width], n_heads, 1.0)
    vv = a[:, 2 * width:]
    gbv = _conv_gate_math(bg_ref[0], al_ref[...], dtb_ref[...])
    og = og_ref[0]
    rows = 16
    rowi = lax.broadcasted_iota(I32, (rows, HEAD_DIM), 0)
    bround = lambda t: t.astype(BF16).astype(F32)
    for h in range(n_heads):
        sl = slice(h * HEAD_DIM, (h + 1) * HEAD_DIM)
        q, k, v = qn[:, sl], kn[:, sl], vv[:, sl]
        beta = gbv[:, h:h + 1]
        eg = jnp.exp(gbv[:, 8 + h:9 + h])
        s = s_ref[0, h]
        lhs = jnp.where(rowi == 0, k * beta * eg, jnp.where(rowi == 1, q * eg, 0.0))
        prod = jnp.dot(lhs.astype(BF16), s.astype(BF16), preferred_element_type=F32)
        v_new = v * beta - prod[0:1, :]
        attn = jnp.sum(bround(q) * bround(k), axis=1, keepdims=True)
        o = prod[1:2, :] + bround(attn) * bround(v_new)
        outer = _tn_dot(jnp.where(rowi == 0, k, 0.0).astype(BF16),
                        jnp.broadcast_to(v_new, (rows, HEAD_DIM)).astype(BF16))
        s_out_ref[0, h] = s * eg + outer
        o = o * lax.rsqrt(jnp.mean(o * o, axis=1, keepdims=True) + NORM_EPS) * nw_ref[...]
        y_ref[0, :, sl] = o * _silu(og[:, sl])


def _dn_step(x3, conv_state, bg3, og3, dn_state, conv_w, al_row, dtb_row, norm_w):
    n_seq, _, width3 = x3.shape
    width = width3 // 3
    n_heads = width // HEAD_DIM
    vec = lambda n: pl.BlockSpec((1, 1, n), lambda s: (s, 0, 0))
    const = lambda r, n: pl.BlockSpec((r, n), lambda s: (0, 0))
    return pl.pallas_call(
        _dn_step_kernel,
        grid=(n_seq,),
        in_specs=[vec(width3),
                  pl.BlockSpec((1, CONV_W - 1, width3), lambda s: (s, 0, 0)),
                  vec(HEAD_DIM), vec(width),
                  pl.BlockSpec((1, n_heads, HEAD_DIM, HEAD_DIM), lambda s: (s, 0, 0, 0)),
                  const(CONV_W, width3), const(1, HEAD_DIM), const(1, HEAD_DIM), const(1, HEAD_DIM)],
        out_specs=[vec(width),
                   pl.BlockSpec((1, n_heads, HEAD_DIM, HEAD_DIM), lambda s: (s, 0, 0, 0)),
                   pl.BlockSpec((1, CONV_W - 1, width3), lambda s: (s, 0, 0))],
        out_shape=[jax.ShapeDtypeStruct((n_seq, 1, width), F32),
                   jax.ShapeDtypeStruct(dn_state.shape, F32),
                   jax.ShapeDtypeStruct(conv_state.shape, F32)],
        compiler_params=_params(("parallel",)),
        name="dn_step",
    )(x3, conv_state, bg3, og3, dn_state, conv_w, al_row, dtb_row, norm_w)


def _top2_sum(a, b, c, d):
    hi1, lo1 = jnp.maximum(a, b), jnp.minimum(a, b)
    hi2, lo2 = jnp.maximum(c, d), jnp.minimum(c, d)
    return jnp.maximum(hi1, hi2) + jnp.maximum(jnp.minimum(hi1, hi2), jnp.maximum(lo1, lo2))


def _outproj_router_kernel(ya_ref, yb_ref, ys_ref, x_ref, w_ref, g_ref, b_ref, wr_ref, rb_ref,
                           x1_ref, info_ref, infot_ref, cnt_ref, carry_ref, mix_ref, *, alpha, n_valid):
    i = pl.program_id(0)
    last = pl.num_programs(0) - 1
    tm = x_ref.shape[0]
    att_w = ya_ref.shape[1]
    dot = functools.partial(jnp.dot, preferred_element_type=F32)

    @pl.when(i == 0)
    def _():
        carry_ref[...] = jnp.zeros(carry_ref.shape, F32)

    @pl.when(i < last)
    def _():
        mix_ref[...] = dot(ya_ref[...], w_ref[:att_w, :]) + dot(yb_ref[...], w_ref[att_w:, :])

    @pl.when(i == last)
    def _():
        mix_ref[...] = dot(ys_ref[...], w_ref[...])

    x1 = _layer_norm(alpha * x_ref[...] + mix_ref[...], g_ref[...], b_ref[...])
    x1_ref[...] = x1

    scores = _sigmoid(_nt_dot(wr_ref[...], x1.astype(BF16)))
    selv = scores + rb_ref[:, 0:1]
    sel = [selv[e:e + 1, :] for e in range(N_EXPERTS)]
    sc = [scores[e:e + 1, :] for e in range(N_EXPERTS)]
    n_groups = N_EXPERTS // EXPERTS_PER_GROUP
    gs = [_top2_sum(*sel[EXPERTS_PER_GROUP * g:EXPERTS_PER_GROUP * (g + 1)]) for g in range(n_groups)]
    best, bidx = gs[0], jnp.zeros((1, tm), I32)
    for g in range(1, n_groups):
        better = gs[g] > best
        best = jnp.where(better, gs[g], best)
        bidx = jnp.where(better, g, bidx)
    ninf = jnp.full((1, tm), -jnp.inf, F32)
    vals = [jnp.where(bidx == e // EXPERTS_PER_GROUP, sel[e], ninf) for e in range(N_EXPERTS)]
    b1, i1, s1 = vals[0], jnp.zeros((1, tm), I32), sc[0]
    for e in range(1, N_EXPERTS):
        better = vals[e] > b1
        b1 = jnp.where(better, vals[e], b1)
        i1 = jnp.where(better, e, i1)
        s1 = jnp.where(better, sc[e], s1)
    b2, i2, s2 = ninf, jnp.zeros((1, tm), I32), jnp.zeros((1, tm), F32)
    for e in range(N_EXPERTS):
        cand = jnp.where(i1 == e, ninf, vals[e])
        better = cand > b2
        b2 = jnp.where(better, cand, b2)
        i2 = jnp.where(better, e, i2)
        s2 = jnp.where(better, sc[e], s2)
    tok = i * tm + lax.broadcasted_iota(I32, (1, tm), 1)
    valid = tok < n_valid
    inv = 1.0 / (s1 + s2)
    g1 = jnp.where(valid, s1 * inv, 0.0)
    g2 = jnp.where(valid, s2 * inv, 0.0)

    erow = lax.broadcasted_iota(I32, (N_EXPERTS, tm), 0)
    oh1 = jnp.where((erow == i1) & valid, 1.0, 0.0)
    oh2 = jnp.where((erow == i2) & valid, 1.0, 0.0)
    both = oh1 + oh2
    upper = jnp.where(lax.broadcasted_iota(I32, (tm, tm), 0) <= lax.broadcasted_iota(I32, (tm, tm), 1), 1.0, 0.0)
    incl = jnp.dot(both.astype(BF16), upper.astype(BF16), preferred_element_type=F32)
    before = carry_ref[:, 0:1] + incl - both
    r1 = jnp.sum(oh1 * before, axis=0, keepdims=True)
    r2 = jnp.sum(oh2 * before, axis=0, keepdims=True)
    carry_ref[...] = carry_ref[...] + incl[:, tm - 1:tm]
    cnt_ref[...] = carry_ref[...]

    row8 = lax.broadcasted_iota(I32, (8, tm), 0)
    info = jnp.zeros((8, tm), F32)
    for r, val in enumerate((i1.astype(F32), i2.astype(F32), g1, g2, r1, r2)):
        info = jnp.where(row8 == r, val, info)
    info_ref[...] = info
    infot_ref[...] = jnp.concatenate([info, jnp.zeros((HEAD_DIM - 8, tm), F32)], axis=0).T


def _outproj_router(ya, yb, ys, x, w_out_bf16, ln_g, ln_b, wr_t, rb_col, alpha, n_valid):
    nt, d = x.shape
    dm = w_out_bf16.shape[0]
    prompt_tiles = ya.shape[0] // ROW_TILE
    assert nt == (prompt_tiles + 1) * ROW_TILE and ys.shape == (ROW_TILE, dm)
    prompt_spec = lambda n: pl.BlockSpec((ROW_TILE, n), lambda i: (jnp.minimum(i, prompt_tiles - 1), 0))
    kern = functools.partial(_outproj_router_kernel, alpha=alpha, n_valid=n_valid)
    return pl.pallas_call(
        kern,
        grid=(nt // ROW_TILE,),
        in_specs=[prompt_spec(ya.shape[1]), prompt_spec(yb.shape[1]),
                  pl.BlockSpec((ROW_TILE, dm), lambda i: (0, 0)),
                  pl.BlockSpec((ROW_TILE, d), lambda i: (i, 0)),
                  pl.BlockSpec((dm, d), lambda i: (0, 0)),
                  pl.BlockSpec((1, d), lambda i: (0, 0)),
                  pl.BlockSpec((1, d), lambda i: (0, 0)),
                  pl.BlockSpec((N_EXPERTS, d), lambda i: (0, 0)),
                  pl.BlockSpec((N_EXPERTS, HEAD_DIM), lambda i: (0, 0))],
        out_specs=[pl.BlockSpec((ROW_TILE, d), lambda i: (i, 0)),
                   pl.BlockSpec((8, ROW_TILE), lambda i: (0, i)),
                   pl.BlockSpec((ROW_TILE, HEAD_DIM), lambda i: (i, 0)),
                   pl.BlockSpec((N_EXPERTS, HEAD_DIM), lambda i: (0, 0))],
        out_shape=[jax.ShapeDtypeStruct((nt, d), F32),
                   jax.ShapeDtypeStruct((8, nt), F32),
                   jax.ShapeDtypeStruct((nt, HEAD_DIM), F32),
                   jax.ShapeDtypeStruct((N_EXPERTS, HEAD_DIM), F32)],
        scratch_shapes=[pltpu.VMEM((N_EXPERTS, HEAD_DIM), F32), pltpu.VMEM((ROW_TILE, d), F32)],
        compiler_params=_params(("arbitrary",)),
        name="outproj_router",
    )(ya, yb, ys, x, w_out_bf16, ln_g, ln_b, wr_t, rb_col)


def _row_copy(src_hbm, src_row, dst_ref, dst_row, sem):
    return pltpu.make_async_copy(src_hbm.at[pl.ds(src_row, 1), :], dst_ref.at[pl.ds(dst_row, 1), :], sem)


def _moe_expert_kernel(rt_ref, be_ref, nu_ref, x_hbm, wg_ref, wu_ref, wd_ref, o_ref, xbuf, sems):
    m = pl.program_id(0)
    n_used = nu_ref[0]
    tm = o_ref.shape[0]

    def issue(block, slot):
        def start(r, carry):
            _row_copy(x_hbm, rt_ref[block * tm + r], xbuf.at[slot], r, sems.at[slot]).start()
            return carry

        lax.fori_loop(0, tm, start, 0, unroll=ROW_DMA_UNROLL)

    def drain(slot):
        def wait(r, carry):
            _row_copy(x_hbm, 0, xbuf.at[slot], r, sems.at[slot]).wait()
            return carry

        lax.fori_loop(0, tm, wait, 0, unroll=ROW_DMA_UNROLL)

    @pl.when(m == 0)
    def _():
        issue(0, 0)

    @pl.when(m + 1 < n_used)
    def _():
        issue(m + 1, (m + 1) % 2)

    @pl.when(m < n_used)
    def _():
        drain(m % 2)
        xb = xbuf[m % 2].astype(BF16)
        hid = _silu(jnp.dot(xb, wg_ref[0], preferred_element_type=F32)) * jnp.dot(
            xb, wu_ref[0], preferred_element_type=F32)
        o_ref[...] = jnp.dot(hid.astype(BF16), wd_ref[0], preferred_element_type=F32)

    @pl.when(m >= n_used)
    def _():
        o_ref[...] = jnp.zeros(o_ref.shape, o_ref.dtype)


def _moe_experts(row_tok, blk_e, n_used, x, wg, wu, wd):
    d = x.shape[1]
    n_rows = row_tok.shape[0]
    f = wg.shape[2]
    return pl.pallas_call(
        _moe_expert_kernel,
        grid_spec=pltpu.PrefetchScalarGridSpec(
            num_scalar_prefetch=3,
            grid=(n_rows // MOE_TILE,),
            in_specs=[pl.BlockSpec(memory_space=pl.ANY),
                      pl.BlockSpec((1, d, f), lambda m, rt, be, nu: (be[m], 0, 0)),
                      pl.BlockSpec((1, d, f), lambda m, rt, be, nu: (be[m], 0, 0)),
                      pl.BlockSpec((1, f, d), lambda m, rt, be, nu: (be[m], 0, 0))],
            out_specs=pl.BlockSpec((MOE_TILE, d), lambda m, rt, be, nu: (m, 0)),
            scratch_shapes=[pltpu.VMEM((2, MOE_TILE, d), F32), pltpu.SemaphoreType.DMA((2,))]),
        out_shape=jax.ShapeDtypeStruct((n_rows, d), F32),
        compiler_params=_params(("arbitrary",)),
        name="moe_experts",
    )(row_tok, blk_e, n_used, x, wg, wu, wd)


def _moe_combine_kernel(dest_ref, y_hbm, x_ref, it_ref, g_ref, b_ref, o_ref, buf_ref, sems, *, alpha, nt):
    i = pl.program_id(0)
    tm = x_ref.shape[0]

    def issue(tile, slot):
        def start(r, carry):
            _row_copy(y_hbm, dest_ref[tile * tm + r], buf_ref.at[slot, 0], r, sems.at[slot]).start()
            _row_copy(y_hbm, dest_ref[nt + tile * tm + r], buf_ref.at[slot, 1], r, sems.at[slot]).start()
            return carry

        lax.fori_loop(0, tm, start, 0, unroll=ROW_DMA_UNROLL)

    def drain(slot):
        def wait(r, carry):
            _row_copy(y_hbm, 0, buf_ref.at[slot, 0], r, sems.at[slot]).wait()
            _row_copy(y_hbm, 0, buf_ref.at[slot, 1], r, sems.at[slot]).wait()
            return carry

        lax.fori_loop(0, tm, wait, 0, unroll=ROW_DMA_UNROLL)

    @pl.when(i == 0)
    def _():
        issue(0, 0)

    @pl.when(i + 1 < pl.num_programs(0))
    def _():
        issue(i + 1, (i + 1) % 2)

    slot = i % 2
    drain(slot)
    it = it_ref[...]
    f = buf_ref[slot, 0] * it[:, 2:3] + buf_ref[slot, 1] * it[:, 3:4]
    o_ref[...] = _layer_norm(alpha * x_ref[...] + f, g_ref[...], b_ref[...])


def _moe_combine(dest, y_sorted, x1, info_t, ln_g, ln_b, alpha):
    nt, d = x1.shape
    kern = functools.partial(_moe_combine_kernel, alpha=alpha, nt=nt)
    return pl.pallas_call(
        kern,
        grid_spec=pltpu.PrefetchScalarGridSpec(
            num_scalar_prefetch=1,
            grid=(nt // ROW_TILE,),
            in_specs=[pl.BlockSpec(memory_space=pl.ANY),
                      pl.BlockSpec((ROW_TILE, d), lambda i, ds: (i, 0)),
                      pl.BlockSpec((ROW_TILE, HEAD_DIM), lambda i, ds: (i, 0)),
                      pl.BlockSpec((1, d), lambda i, ds: (0, 0)),
                      pl.BlockSpec((1, d), lambda i, ds: (0, 0))],
            out_specs=pl.BlockSpec((ROW_TILE, d), lambda i, ds: (i, 0)),
            scratch_shapes=[pltpu.VMEM((2, 2, ROW_TILE, d), F32), pltpu.SemaphoreType.DMA((2,))]),
        out_shape=jax.ShapeDtypeStruct((nt, d), F32),
        compiler_params=_params(("arbitrary",)),
        name="moe_combine",
    )(dest, y_sorted, x1, info_t, ln_g, ln_b)


def _moe_layer(ya, yb, ys, x, w_out_bf16, ln1_g, ln1_b, wr_t, rb_col, wg, wu, wd, first_expert, ln2_g, ln2_b,
               alpha, n_valid):
    nt, d = x.shape
    x1, info, info_t, cnt = _outproj_router(ya, yb, ys, x, w_out_bf16, ln1_g, ln1_b, wr_t, rb_col, alpha, n_valid)
    n_blocks = -(-(2 * n_valid) // MOE_TILE) + N_EXPERTS
    counts = cnt[:, 0].astype(I32)
    padded = (counts + MOE_TILE - 1) // MOE_TILE * MOE_TILE
    pad_end = jnp.cumsum(padded)
    pad_start = pad_end - padded
    e1, e2 = info[0].astype(I32), info[1].astype(I32)
    dest1 = pad_start[e1] + info[4].astype(I32)
    dest2 = pad_start[e2] + info[5].astype(I32)
    tok = jnp.arange(n_valid, dtype=I32)
    row_tok = jnp.zeros((n_blocks * MOE_TILE,), I32).at[dest1[:n_valid]].set(tok).at[dest2[:n_valid]].set(tok)
    n_used = (pad_end[-1] // MOE_TILE).astype(I32)
    blk_start = jnp.arange(n_blocks, dtype=I32) * MOE_TILE
    blk_e = jnp.minimum(jnp.sum(pad_end[None, :] <= blk_start[:, None], axis=1), N_EXPERTS - 1).astype(I32)
    blk_e = jnp.where(jnp.arange(n_blocks) < n_used, blk_e, blk_e[jnp.maximum(n_used - 1, 0)])
    valid = jnp.arange(nt) < n_valid
    dest = jnp.concatenate([jnp.where(valid, dest1, 0), jnp.where(valid, dest2, 0)]).astype(I32)
    nu = n_used.reshape(1)
    ys = _moe_experts(row_tok, blk_e + first_expert, nu, x1, wg, wu, wd)
    return _moe_combine(dest, ys, x1, info_t, ln2_g, ln2_b, alpha)


def kernel(x_prompt, x_sample, cache_k, cache_v, state_dn, state_conv, page_table, ln_in_g, ln_in_b, rel_bias,
           w_router, router_bias, w_in, w_out, conv_w, a_log, dt_bias, dn_norm_w, ln1_g, ln1_b, ln2_g, ln2_b,
           w_gate, w_up, w_down):
    batch, seq, d = x_prompt.shape
    n_seq, dec_seq, _ = x_sample.shape
    depth = w_in.shape[0]
    n_pool, page = cache_k.shape[1], cache_k.shape[2]
    kvh = cache_k.shape[3]
    n_pages = page_table.shape[1]
    past_len = n_pages * page
    assert dec_seq == 1 and past_len % MOBA_BLOCK == 0 and n_seq <= ROW_TILE
    att_w = d // 2
    dn_w = d - att_w
    heads = att_w // HEAD_DIM
    dn_heads = dn_w // HEAD_DIM
    kv_w = kvh * HEAD_DIM
    assert dn_heads == 8 and heads == ATT_GROUP * kvh
    alpha = (2 * depth) ** 0.25
    n_prompt = batch * seq
    nt = n_prompt + ROW_TILE
    n_valid = n_prompt + n_seq

    x = jnp.concatenate([x_prompt.reshape(n_prompt, d), x_sample.reshape(n_seq, d),
                         jnp.zeros((ROW_TILE - n_seq, d), F32)], axis=0)
    x = _input_layer_norm(x, ln_in_g, ln_in_b)

    widths = (att_w, kv_w, kv_w, 3 * dn_w, dn_w, HEAD_DIM)
    in_cols = w_in.shape[2]
    w_in_b = jnp.pad(w_in, ((0, 0), (0, 0), (0, sum(widths) - in_cols))).astype(BF16)
    w_out_b = w_out.astype(BF16)
    all_experts = lambda w: w.astype(BF16).reshape((depth * N_EXPERTS,) + w.shape[2:])
    wg_b, wu_b, wd_b = all_experts(w_gate), all_experts(w_up), all_experts(w_down)
    wr_t = w_router.T.astype(BF16)
    rb_col = jnp.broadcast_to(router_bias.astype(F32)[:, None], (N_EXPERTS, HEAD_DIM))
    lane_pad = lambda v: jnp.pad(v.astype(F32), ((0, 0), (dn_heads, HEAD_DIM - 2 * dn_heads)))
    al_rows, dtb_rows = lane_pad(a_log), lane_pad(dt_bias)
    bias_tiles = _bias_tiles(rel_bias, seq // MOBA_BLOCK)
    pt = page_table.astype(I32)

    ks_p, vs_p, dn_p, cv_p, ks_s, vs_s, dn_s, cv_s = [], [], [], [], [], [], [], []
    for l in range(depth):
        qa, ka, va, qkvb, og, bg = _in_projection(x, w_in_b[l], widths)
        al_row, dtb_row = al_rows[l:l + 1], dtb_rows[l:l + 1]
        norm_w = dn_norm_w[l].reshape(1, HEAD_DIM).astype(F32)

        ya_p = _moba_prompt(qa, ka, va, bias_tiles, batch, seq)
        qn, kn, vv, gb = _dn_prep(qkvb, bg, conv_w[l], al_row, dtb_row, batch, seq)
        yb_p, s_p = _dn_chunks(qn, kn, vv, gb, og, norm_w, batch, seq)

        sl = slice(n_prompt, n_prompt + n_seq)
        q3 = qa[sl].reshape(n_seq, heads, HEAD_DIM)
        kn3 = ka[sl].reshape(n_seq, kvh, HEAD_DIM)
        vn3 = va[sl].reshape(n_seq, kvh, HEAD_DIM)
        ksum = _past_block_sums(cache_k, l, pt)
        top = _decode_select(q3, ksum.transpose(0, 2, 1, 3))[:, :, :MOBA_TOPK]
        ya_s = _decode_attend(q3, kn3, vn3, cache_k, cache_v, l, pt, top, rel_bias, past_len)
        yb_s, s_s, c_s = _dn_step(qkvb[sl].reshape(n_seq, 1, 3 * dn_w), state_conv[l],
                                  bg[sl].reshape(n_seq, 1, HEAD_DIM), og[sl].reshape(n_seq, 1, dn_w),
                                  state_dn[l], conv_w[l], al_row, dtb_row, norm_w)

        y_s = jnp.concatenate([ya_s.reshape(n_seq, att_w), yb_s.reshape(n_seq, dn_w)], axis=1).astype(BF16)
        y_s = jnp.pad(y_s, ((0, ROW_TILE - n_seq), (0, 0)))
        x = _moe_layer(ya_p, yb_p, y_s, x, w_out_b[l], ln1_g[l].reshape(1, d), ln1_b[l].reshape(1, d), wr_t, rb_col,
                       wg_b, wu_b, wd_b, l * N_EXPERTS, ln2_g[l].reshape(1, d), ln2_b[l].reshape(1, d),
                       alpha, n_valid)

        ks_p.append(ka)
        vs_p.append(va)
        dn_p.append(s_p)
        cv_p.append(qkvb[:n_prompt].reshape(batch, seq, 3 * dn_w)[:, seq - (CONV_W - 1):])
        ks_s.append(kn3.reshape(n_seq, 1, kvh, HEAD_DIM))
        vs_s.append(vn3.reshape(n_seq, 1, kvh, HEAD_DIM))
        dn_s.append(s_s)
        cv_s.append(c_s)

    y_prompt = x[:n_prompt].reshape(batch, seq, d)
    y_sample = x[n_prompt:n_prompt + n_seq].reshape(n_seq, 1, d)
    prompt_kv = lambda rows: jnp.stack([r[:n_prompt] for r in rows]).reshape(depth, batch, seq, kvh, HEAD_DIM)
    return (y_prompt, y_sample, prompt_kv(ks_p), prompt_kv(vs_p), jnp.stack(dn_p), jnp.stack(cv_p),
            jnp.stack(ks_s), jnp.stack(vs_s), jnp.stack(dn_s), jnp.stack(cv_s))
```

```python
import functools
import math

import jax
import jax.numpy as jnp
from jax import lax
from jax.experimental import pallas as pl
from jax.experimental.pallas import tpu as pltpu

F32 = jnp.float32
BF16 = jnp.bfloat16
I32 = jnp.int32
HIGHEST = lax.Precision.HIGHEST

HEAD_DIM = 128
ATT_GROUP = 2
MOBA_BLOCK = 256
MOBA_TOPK = 3
MOBA_Q_SUB = 256
REL_BUCKETS = 32
REL_MAX_DIST = 2048
DN_CHUNK = 64
CONV_W = 4
N_EXPERTS = 16
EXPERTS_PER_GROUP = 4
LN_EPS = 1e-5
NORM_EPS = 1e-6
ROW_TILE = 256
MOE_TILE = 256
ROW_DMA_UNROLL = 8
PAGES_PER_STEP = 16
NEG = -1e30
VMEM_LIMIT_V7X = 56 * 1024 * 1024


def _params(semantics):
    return pltpu.CompilerParams(dimension_semantics=semantics, vmem_limit_bytes=VMEM_LIMIT_V7X)


def _nt_dot(a, b, precision=None):
    return lax.dot_general(a, b, (((1,), (1,)), ((), ())), precision=precision, preferred_element_type=F32)


def _tn_dot(a, b, precision=None):
    return lax.dot_general(a, b, (((0,), (0,)), ((), ())), precision=precision, preferred_element_type=F32)


def _hdot(a, b):
    return jnp.dot(a, b, precision=HIGHEST, preferred_element_type=F32)


def _bdot(a, b):
    return jnp.dot(a.astype(BF16), b.astype(BF16), preferred_element_type=F32)


def _dot3(a, b):
    ah, bh = a.astype(BF16), b.astype(BF16)
    al, bl = (a - ah.astype(F32)).astype(BF16), (b - bh.astype(F32)).astype(BF16)
    dot = functools.partial(jnp.dot, preferred_element_type=F32)
    return dot(ah, bh) + (dot(ah, bl) + dot(al, bh))


def _layer_norm(x, g, b):
    mu = jnp.mean(x, axis=-1, keepdims=True)
    xc = x - mu
    var = jnp.mean(xc * xc, axis=-1, keepdims=True)
    return xc * lax.rsqrt(var + LN_EPS) * g + b


def _silu(x):
    return x * (1.0 / (1.0 + jnp.exp(-x)))


def _sigmoid(x):
    return 1.0 / (1.0 + jnp.exp(-x))


def _softplus(x):
    return jnp.maximum(x, 0.0) + jnp.log(1.0 + jnp.exp(-jnp.abs(x)))


def _rel_bucket(dist):
    n = jnp.maximum(dist, 0)
    max_exact = REL_BUCKETS // 2
    nf = jnp.maximum(n, 1).astype(F32)
    large = max_exact + (jnp.log(nf / max_exact) / math.log(REL_MAX_DIST / max_exact)
                         * (REL_BUCKETS - max_exact)).astype(I32)
    large = jnp.minimum(large, REL_BUCKETS - 1)
    return jnp.where(n < max_exact, n, large)


def _ln_kernel(x_ref, g_ref, b_ref, o_ref):
    o_ref[...] = _layer_norm(x_ref[...], g_ref[...], b_ref[...])


def _input_layer_norm(x, g, b):
    nt, d = x.shape
    return pl.pallas_call(
        _ln_kernel,
        grid=(nt // ROW_TILE,),
        in_specs=[pl.BlockSpec((ROW_TILE, d), lambda i: (i, 0)),
                  pl.BlockSpec((1, d), lambda i: (0, 0)),
                  pl.BlockSpec((1, d), lambda i: (0, 0))],
        out_specs=pl.BlockSpec((ROW_TILE, d), lambda i: (i, 0)),
        out_shape=jax.ShapeDtypeStruct((nt, d), F32),
        compiler_params=_params(("parallel",)),
        name="input_layer_norm",
    )(x, g.reshape(1, d), b.reshape(1, d))


def _inproj_kernel(x_ref, w_ref, *o_refs):
    xb = x_ref[...].astype(BF16)
    off = 0
    for o_ref in o_refs:
        n = o_ref.shape[1]
        o_ref[...] = jnp.dot(xb, w_ref[0, :, off:off + n], preferred_element_type=F32)
        off += n


def _in_projection(x, w_bf16, layer, widths):
    nt, d = x.shape
    cols = w_bf16.shape[2]
    assert sum(widths) == cols
    return pl.pallas_call(
        _inproj_kernel,
        grid=(nt // ROW_TILE,),
        in_specs=[pl.BlockSpec((ROW_TILE, d), lambda i: (i, 0)),
                  pl.BlockSpec((1, d, cols), lambda i: (layer, 0, 0), pipeline_mode=pl.Buffered(1))],
        out_specs=[pl.BlockSpec((ROW_TILE, n), lambda i: (i, 0)) for n in widths],
        out_shape=[jax.ShapeDtypeStruct((nt, n), F32) for n in widths],
        compiler_params=_params(("parallel",)),
        name="in_projection",
    )(x, w_bf16)


def _bias_tile_kernel(tab_ref, o_ref):
    h = pl.program_id(0)
    d = pl.program_id(1)
    key = lax.broadcasted_iota(I32, (MOBA_BLOCK, MOBA_BLOCK), 0)
    qry = lax.broadcasted_iota(I32, (MOBA_BLOCK, MOBA_BLOCK), 1)
    dist = d * MOBA_BLOCK + qry - key
    bucket = _rel_bucket(dist)
    acc = jnp.zeros((MOBA_BLOCK, MOBA_BLOCK), F32)
    for k in range(REL_BUCKETS):
        acc = jnp.where(bucket == k, tab_ref[k, h], acc)
    o_ref[0, 0] = jnp.where(dist >= 0, acc, NEG)


def _bias_tiles(rel_bias, n_blk):
    heads = rel_bias.shape[1]
    return pl.pallas_call(
        _bias_tile_kernel,
        grid=(heads, n_blk),
        in_specs=[pl.BlockSpec(memory_space=pltpu.SMEM)],
        out_specs=pl.BlockSpec((1, 1, MOBA_BLOCK, MOBA_BLOCK), lambda h, d: (h, d, 0, 0)),
        out_shape=jax.ShapeDtypeStruct((heads, n_blk, MOBA_BLOCK, MOBA_BLOCK), F32),
        compiler_params=_params(("parallel", "parallel")),
        name="bias_tiles",
    )(rel_bias.astype(F32))


def _moba_prompt_kernel(q_ref, k_ref, v_ref, bias_ref, o_ref, kaug_ref, vt_ref, kmean_ref):
    i = pl.program_id(2)
    seq = k_ref.shape[0]
    n_blk = seq // MOBA_BLOCK
    scale = HEAD_DIM ** -0.5

    @pl.when(i == 0)
    def _():
        kaug_ref[:, :HEAD_DIM] = k_ref[...].astype(BF16)
        rows = lax.broadcasted_iota(I32, (seq, HEAD_DIM), 0) // MOBA_BLOCK
        lanes = lax.broadcasted_iota(I32, (seq, HEAD_DIM), 1)
        kaug_ref[:, HEAD_DIM:] = jnp.where(rows == lanes, 1.0, 0.0).astype(BF16)
        for j in range(n_blk):
            blk = slice(j * MOBA_BLOCK, (j + 1) * MOBA_BLOCK)
            vt_ref[j] = v_ref[blk, :].T.astype(BF16)
            kmean_ref[j:j + 1, :] = jnp.mean(k_ref[blk, :], axis=0, keepdims=True)

    tq = q_ref.shape[0]
    kmean = kmean_ref[...].astype(BF16)
    chains = []
    for hh in range(ATT_GROUP):
        q = q_ref[:, hh * HEAD_DIM:(hh + 1) * HEAD_DIM]
        gate = _nt_dot(kmean, q.astype(BF16))
        jidx = lax.broadcasted_iota(I32, gate.shape, 0)
        rank = jnp.zeros(gate.shape, I32)
        for jp in range(n_blk):
            row = gate[jp:jp + 1, :]
            before = (row > gate) | ((row == gate) & (jp < jidx))
            rank = rank + jnp.where(before, jnp.where(jp < i, 1, 0), 0)
        sel = ((jidx < i) & (rank < MOBA_TOPK)) | (jidx == i)
        selneg = jnp.where(sel, 0.0, NEG)
        selneg = jnp.concatenate([selneg, jnp.zeros((HEAD_DIM - n_blk, tq), F32)], axis=0)
        q_aug_t = jnp.concatenate([q.T.astype(BF16), selneg.astype(BF16)], axis=0)
        for c0 in range(0, tq, MOBA_Q_SUB):
            chains.append((hh, c0, q_aug_t[:, c0:c0 + MOBA_Q_SUB]))

    dot = functools.partial(jnp.dot, preferred_element_type=F32)

    def logits(t):
        j = jnp.maximum(i - t, 0)
        tb = jnp.minimum(t, n_blk - 1)
        kj = kaug_ref[pl.ds(pl.multiple_of(j * MOBA_BLOCK, MOBA_BLOCK), MOBA_BLOCK), :]
        return tuple(dot(kj, qc) * scale + bias_ref[hh, tb, :, c0:c0 + MOBA_Q_SUB] for hh, c0, qc in chains)

    def body(t, carry):
        ss, stats = carry
        ss_next = logits(t + 1)
        vtj = vt_ref[i - t]
        m_new = [jnp.maximum(c[0], jnp.max(s, axis=0, keepdims=True)) for c, s in zip(stats, ss)]
        ps = [jnp.exp(s - mn) for s, mn in zip(ss, m_new)]
        alphas = [jnp.exp(c[0] - mn) for c, mn in zip(stats, m_new)]
        pvs = [dot(vtj, p.astype(BF16)) for p in ps]
        ls = [a * c[1] + jnp.sum(p, axis=0, keepdims=True) for a, c, p in zip(alphas, stats, ps)]
        accs = [a * c[2] + pv for a, c, pv in zip(alphas, stats, pvs)]
        return ss_next, tuple(zip(m_new, ls, accs))

    init = tuple((jnp.full((1, MOBA_Q_SUB), -jnp.inf, F32), jnp.zeros((1, MOBA_Q_SUB), F32),
                  jnp.zeros((HEAD_DIM, MOBA_Q_SUB), F32)) for _ in chains)
    _, final = lax.fori_loop(0, i + 1, body, (logits(0), init))
    for (hh, c0, _), (_, l, acc) in zip(chains, final):
        o_ref[c0:c0 + MOBA_Q_SUB, hh * HEAD_DIM:(hh + 1) * HEAD_DIM] = (acc / l).T.astype(o_ref.dtype)


def _moba_prompt(qa, ka, va, bias_tiles, batch, seq):
    kvh = ka.shape[1] // HEAD_DIM
    group_w = ATT_GROUP * HEAD_DIM
    n_blk = seq // MOBA_BLOCK
    assert seq % MOBA_BLOCK == 0 and n_blk <= HEAD_DIM
    return pl.pallas_call(
        _moba_prompt_kernel,
        grid=(kvh, batch, n_blk),
        in_specs=[pl.BlockSpec((MOBA_BLOCK, group_w), lambda g, b, i: (b * n_blk + i, g)),
                  pl.BlockSpec((seq, HEAD_DIM), lambda g, b, i: (b, g)),
                  pl.BlockSpec((seq, HEAD_DIM), lambda g, b, i: (b, g)),
                  pl.BlockSpec((ATT_GROUP, n_blk, MOBA_BLOCK, MOBA_BLOCK), lambda g, b, i: (g, 0, 0, 0))],
        out_specs=pl.BlockSpec((MOBA_BLOCK, group_w), lambda g, b, i: (b * n_blk + i, g)),
        out_shape=jax.ShapeDtypeStruct((batch * seq, kvh * group_w), BF16),
        scratch_shapes=[pltpu.VMEM((seq, 2 * HEAD_DIM), BF16),
                        pltpu.VMEM((n_blk, HEAD_DIM, MOBA_BLOCK), BF16),
                        pltpu.VMEM((n_blk, HEAD_DIM), F32)],
        compiler_params=_params(("parallel", "parallel", "arbitrary")),
        name="moba_prompt",
    )(qa, ka, va, bias_tiles)


def _page_sum_kernel(pt_ref, *refs):
    o_ref = refs[-1]
    for u in range(0, PAGES_PER_STEP, 2):
        o_ref[0, u // 2] = jnp.sum(refs[u][0, 0], axis=0) + jnp.sum(refs[u + 1][0, 0], axis=0)


def _past_block_sums(cache, layer, page_table):
    n_seq, n_pages = page_table.shape
    page, kvh = cache.shape[2], cache.shape[3]
    assert 2 * page == MOBA_BLOCK and n_pages % PAGES_PER_STEP == 0
    steps = n_pages // PAGES_PER_STEP
    blk_per_step = PAGES_PER_STEP // 2

    def page_spec(u):
        return pl.BlockSpec((1, 1, page, kvh, HEAD_DIM),
                            lambda s, t, pt: (layer, pt[s, t * PAGES_PER_STEP + u], 0, 0, 0))

    return pl.pallas_call(
        _page_sum_kernel,
        grid_spec=pltpu.PrefetchScalarGridSpec(
            num_scalar_prefetch=1,
            grid=(n_seq, steps),
            in_specs=[page_spec(u) for u in range(PAGES_PER_STEP)],
            out_specs=pl.BlockSpec((1, blk_per_step, kvh, HEAD_DIM), lambda s, t, pt: (s, t, 0, 0))),
        out_shape=jax.ShapeDtypeStruct((n_seq, n_pages // 2, kvh, HEAD_DIM), F32),
        compiler_params=_params(("parallel", "arbitrary")),
        name="past_block_sums",
    )(page_table, *([cache] * PAGES_PER_STEP))


def _decode_select_kernel(q_ref, ksum_ref, o_ref):
    heads = q_ref.shape[1]
    n_blk = ksum_ref.shape[2]
    q = q_ref[0]
    qb = jnp.concatenate([q, q], axis=0).astype(BF16)
    hrow = lax.broadcasted_iota(I32, (heads, n_blk), 0)
    gate = jnp.zeros((heads, n_blk), F32)
    for g in range(heads // ATT_GROUP):
        kmean = ksum_ref[0, g] * (1.0 / MOBA_BLOCK)
        gate = jnp.where(hrow // ATT_GROUP == g, _nt_dot(qb, kmean.astype(BF16))[:heads], gate)
    nidx = lax.broadcasted_iota(I32, (heads, n_blk), 1)
    rank = jnp.zeros((heads, n_blk), I32)
    for n in range(n_blk):
        col = gate[:, n:n + 1]
        rank = rank + jnp.where((col > gate) | ((col == gate) & (n < nidx)), 1, 0)
    lane = lax.broadcasted_iota(I32, (heads, HEAD_DIM), 1)
    out = jnp.zeros((heads, HEAD_DIM), I32)
    for r in range(MOBA_TOPK):
        idx = jnp.sum(jnp.where(rank == r, nidx.astype(F32), 0.0), axis=1, keepdims=True)
        out = jnp.where(lane == r, idx.astype(I32), out)
    o_ref[0] = out


def _decode_select(q3, ksum):
    n_seq, heads, _ = q3.shape
    kvh, n_blk = ksum.shape[1], ksum.shape[2]
    return pl.pallas_call(
        _decode_select_kernel,
        grid=(n_seq,),
        in_specs=[pl.BlockSpec((1, heads, HEAD_DIM), lambda s: (s, 0, 0)),
                  pl.BlockSpec((1, kvh, n_blk, HEAD_DIM), lambda s: (s, 0, 0, 0))],
        out_specs=pl.BlockSpec((1, heads, HEAD_DIM), lambda s: (s, 0, 0)),
        out_shape=jax.ShapeDtypeStruct((n_seq, heads, HEAD_DIM), I32),
        compiler_params=_params(("parallel",)),
        name="decode_select",
    )(q3, ksum)


def _decode_attend_kernel(pt_ref, blk_ref, q_ref, kn_ref, vn_ref, tab_ref, ck_hbm, cv_hbm, o_ref,
                          kbuf, vbuf, sems, *, past_len, layer):
    n_tiles = 2 * MOBA_TOPK
    s = pl.program_id(0)
    heads = q_ref.shape[1]
    page = kbuf.shape[2]
    scale = HEAD_DIM ** -0.5
    rows = 16
    bround = lambda a: a.astype(BF16).astype(F32)
    lane = lax.broadcasted_iota(I32, (1, page), 1)

    def tile_copies(h):
        g = h // ATT_GROUP
        copies = []
        for u in range(n_tiles):
            pg = pt_ref[s, 2 * blk_ref[s * heads + h, u // 2] + (u % 2)]
            copies.append(pltpu.make_async_copy(ck_hbm.at[layer, pg, :, g, :], kbuf.at[h, u], sems.at[h]))
            copies.append(pltpu.make_async_copy(cv_hbm.at[layer, pg, :, g, :], vbuf.at[h, u], sems.at[h]))
        return copies

    for h in range(heads):
        for cp in tile_copies(h):
            cp.start()

    for h in range(heads):
        for cp in tile_copies(h):
            cp.wait()
        q = jnp.broadcast_to(q_ref[0, h:h + 1, :], (rows, HEAD_DIM)).astype(BF16)
        logits = []
        for u in range(n_tiles):
            kpos = blk_ref[s * heads + h, u // 2] * MOBA_BLOCK + (u % 2) * page + lane
            bucket = _rel_bucket(past_len - kpos)
            bias = jnp.zeros((1, page), F32)
            for k in range(REL_BUCKETS):
                bias = jnp.where(bucket == k, tab_ref[k, h], bias)
            logits.append(_nt_dot(q, kbuf[h, u].astype(BF16))[0:1, :] * scale + bias)
        g = h // ATT_GROUP
        kn = kn_ref[0, g:g + 1, :]
        vn = vn_ref[0, g:g + 1, :]
        own = jnp.sum(q[0:1, :].astype(F32) * bround(kn), axis=1, keepdims=True) * scale + tab_ref[0, h]

        m = own
        for lg in logits:
            m = jnp.maximum(m, jnp.max(lg, axis=1, keepdims=True))
        e_own = jnp.exp(own - m)
        es = [jnp.exp(lg - m) for lg in logits]
        den = e_own
        for e in es:
            den = den + jnp.sum(e, axis=1, keepdims=True)
        acc = bround(e_own / den) * bround(vn)
        for u in range(n_tiles):
            p = jnp.broadcast_to(es[u] / den, (rows, page)).astype(BF16)
            acc = acc + jnp.dot(p, vbuf[h, u].astype(BF16), preferred_element_type=F32)[0:1, :]
        o_ref[0, h:h + 1, :] = acc


def _decode_attend(q3, kn3, vn3, cache_k, cache_v, layer, page_table, blocks, rel_bias, past_len):
    n_seq, heads, _ = q3.shape
    kvh = kn3.shape[1]
    page = cache_k.shape[2]
    n_tiles = 2 * MOBA_TOPK
    return pl.pallas_call(
        functools.partial(_decode_attend_kernel, past_len=past_len, layer=layer),
        grid_spec=pltpu.PrefetchScalarGridSpec(
            num_scalar_prefetch=2,
            grid=(n_seq,),
            in_specs=[pl.BlockSpec((1, heads, HEAD_DIM), lambda s, pt, bk: (s, 0, 0)),
                      pl.BlockSpec((1, kvh, HEAD_DIM), lambda s, pt, bk: (s, 0, 0)),
                      pl.BlockSpec((1, kvh, HEAD_DIM), lambda s, pt, bk: (s, 0, 0)),
                      pl.BlockSpec(memory_space=pltpu.SMEM),
                      pl.BlockSpec(memory_space=pl.ANY),
                      pl.BlockSpec(memory_space=pl.ANY)],
            out_specs=pl.BlockSpec((1, heads, HEAD_DIM), lambda s, pt, bk: (s, 0, 0)),
            scratch_shapes=[pltpu.VMEM((heads, n_tiles, page, HEAD_DIM), F32),
                            pltpu.VMEM((heads, n_tiles, page, HEAD_DIM), F32),
                            pltpu.SemaphoreType.DMA((heads,))]),
        out_shape=jax.ShapeDtypeStruct((n_seq, heads, HEAD_DIM), F32),
        compiler_params=_params(("arbitrary",)),
        name="decode_attend",
    )(page_table, blocks.reshape(n_seq * heads, MOBA_TOPK), q3, kn3, vn3, rel_bias.astype(F32), cache_k, cache_v)


def _conv_gate_math(bg, al, dtb):
    lane = lax.broadcasted_iota(I32, bg.shape, 1)
    return jnp.where(lane < 8, _sigmoid(bg), -jnp.exp(al) * _softplus(bg + dtb))


def _head_l2norm(x, n_heads, mult):
    outs = []
    for h in range(n_heads):
        seg = x[:, h * HEAD_DIM:(h + 1) * HEAD_DIM]
        ss = jnp.sum(seg * seg, axis=1, keepdims=True)
        outs.append(seg * (lax.rsqrt(ss + NORM_EPS) * mult))
    return jnp.concatenate(outs, axis=1)


def _dn_prep_kernel(cur_ref, prev_ref, bg_ref, cw_ref, al_ref, dtb_ref, qn_ref, kn_ref, vv_ref, gb_ref):
    i = pl.program_id(1)
    t, width3 = cur_ref.shape
    width = width3 // 3
    n_heads = width // HEAD_DIM
    cur = cur_ref[...]
    prev = prev_ref[...] * jnp.where(i > 0, 1.0, 0.0)
    row8 = lax.broadcasted_iota(I32, (8, 1), 0)
    y = cur * cw_ref[CONV_W - 1:CONV_W, :]
    for k in range(1, CONV_W):
        rolled = pltpu.roll(cur, k, 0)
        head = jnp.where(row8 < k, pltpu.roll(prev, k, 0), rolled[:8])
        shifted = jnp.concatenate([head, rolled[8:]], axis=0)
        y = y + shifted * cw_ref[CONV_W - 1 - k:CONV_W - k, :]
    a = _silu(y)
    qn_ref[...] = _head_l2norm(a[:, :width], n_heads, HEAD_DIM ** -0.5)
    kn_ref[...] = _head_l2norm(a[:, width:2 * width], n_heads, 1.0)
    vv_ref[...] = a[:, 2 * width:]
    gb_ref[...] = _conv_gate_math(bg_ref[...], al_ref[...], dtb_ref[...])


def _dn_prep(qkvb, bg, conv_w, al_row, dtb_row, batch, seq):
    width3 = qkvb.shape[1]
    width = width3 // 3
    tiles = seq // ROW_TILE
    rows = batch * seq
    row_spec = lambda n: pl.BlockSpec((ROW_TILE, n), lambda b, i: (b * tiles + i, 0))
    return pl.pallas_call(
        _dn_prep_kernel,
        grid=(batch, tiles),
        in_specs=[row_spec(width3),
                  pl.BlockSpec((8, width3), lambda b, i: (jnp.maximum((b * tiles + i) * (ROW_TILE // 8) - 1, 0), 0)),
                  row_spec(HEAD_DIM),
                  pl.BlockSpec((CONV_W, width3), lambda b, i: (0, 0)),
                  pl.BlockSpec((1, HEAD_DIM), lambda b, i: (0, 0)),
                  pl.BlockSpec((1, HEAD_DIM), lambda b, i: (0, 0))],
        out_specs=[row_spec(width), row_spec(width), row_spec(width), row_spec(HEAD_DIM)],
        out_shape=[jax.ShapeDtypeStruct((rows, width), F32)] * 3 + [jax.ShapeDtypeStruct((rows, HEAD_DIM), F32)],
        compiler_params=_params(("parallel", "parallel")),
        name="dn_prep",
    )(qkvb, qkvb, bg, conv_w, al_row, dtb_row)


def _dn_chunk_kernel(qn_ref, kn_ref, vv_ref, gb_ref, og_ref, nw_ref, y_ref, s_out_ref, s_ref):
    c = pl.program_id(1)
    n_chunks = pl.num_programs(1)
    ch, width = qn_ref.shape
    n_heads = width // HEAD_DIM

    @pl.when(c == 0)
    def _():
        s_ref[...] = jnp.zeros(s_ref.shape, F32)

    ri = lax.broadcasted_iota(I32, (ch, ch), 0)
    ci = lax.broadcasted_iota(I32, (ch, ch), 1)
    tri = ri >= ci
    strict = ri > ci
    eye = ri == ci
    gbv = gb_ref[...]
    gcum_all = _hdot(jnp.where(tri, 1.0, 0.0), gbv)

    hs = range(n_heads)
    sls = [slice(h * HEAD_DIM, (h + 1) * HEAD_DIM) for h in hs]
    dot = functools.partial(jnp.dot, preferred_element_type=F32)
    q = [qn_ref[:, sl] for sl in sls]
    k = [kn_ref[:, sl] for sl in sls]
    beta = [gbv[:, h:h + 1] for h in hs]
    gc = [gcum_all[:, 8 + h:9 + h] for h in hs]
    gcb = [jnp.broadcast_to(g, (ch, ch)) for g in gc]
    grow = [jnp.sum(jnp.where(eye, g, 0.0), axis=0, keepdims=True) for g in gcb]
    decay = [jnp.where(tri, jnp.exp(jnp.where(tri, g - r, 0.0)), 0.0) for g, r in zip(gcb, grow)]
    kbeta = [kk * b for kk, b in zip(k, beta)]
    egc = [jnp.exp(g) for g in gc]
    k16 = [kk.astype(BF16) for kk in k]
    a = [jnp.where(strict, _nt_dot(kb.astype(BF16), kk) * d, 0.0) for kb, kk, d in zip(kbeta, k16, decay)]
    x = [jnp.concatenate([vv_ref[:, sl] * b, kb * e], axis=1) for sl, b, kb, e in zip(sls, beta, kbeta, egc)]
    npow = [-aa for aa in a]
    span = 1
    while True:
        mm = _dot3 if span <= 2 else _bdot
        x = [xx + mm(n, xx) for n, xx in zip(npow, x)]
        span *= 2
        if span >= ch:
            break
        npow = [mm(n, n) for n in npow]
    s = [s_ref[h] for h in hs]
    s16 = [ss.astype(BF16) for ss in s]
    v_new = [xx[:, :HEAD_DIM] - dot(xx[:, HEAD_DIM:].astype(BF16), ss) for xx, ss in zip(x, s16)]
    v16 = [vn.astype(BF16) for vn in v_new]
    attn = [jnp.where(tri, _nt_dot(qq.astype(BF16), kk) * d, 0.0) for qq, kk, d in zip(q, k16, decay)]
    o = [dot((qq * e).astype(BF16), ss) + dot(at.astype(BF16), vn)
         for qq, e, ss, at, vn in zip(q, egc, s16, attn, v16)]
    g_last = [g[ch - 1:ch, :] for g in gc]
    kdec = [(kk * jnp.exp(gl - g)).astype(BF16) for kk, gl, g in zip(k, g_last, gc)]
    s_new = [ss * jnp.exp(gl) + _tn_dot(kd, vn) for ss, gl, kd, vn in zip(s, g_last, kdec, v16)]
    for h in hs:
        s_ref[h] = s_new[h]
    for h, sl in enumerate(sls):
        on = o[h] * lax.rsqrt(jnp.mean(o[h] * o[h], axis=1, keepdims=True) + NORM_EPS) * nw_ref[...]
        y_ref[:, sl] = (on * _silu(og_ref[:, sl])).astype(y_ref.dtype)

    @pl.when(c == n_chunks - 1)
    def _():
        s_out_ref[0] = s_ref[...]


def _dn_chunks(qn, kn, vv, gb, og, norm_w, batch, seq):
    width = qn.shape[1]
    n_heads = width // HEAD_DIM
    assert seq % DN_CHUNK == 0
    n_chunks = seq // DN_CHUNK
    row_spec = lambda n: pl.BlockSpec((DN_CHUNK, n), lambda b, c: (b * n_chunks + c, 0))
    return pl.pallas_call(
        _dn_chunk_kernel,
        grid=(batch, n_chunks),
        in_specs=[row_spec(width), row_spec(width), row_spec(width), row_spec(HEAD_DIM), row_spec(width),
                  pl.BlockSpec((1, HEAD_DIM), lambda b, c: (0, 0))],
        out_specs=[row_spec(width),
                   pl.BlockSpec((1, n_heads, HEAD_DIM, HEAD_DIM), lambda b, c: (b, 0, 0, 0))],
        out_shape=[jax.ShapeDtypeStruct((batch * seq, width), BF16),
                   jax.ShapeDtypeStruct((batch, n_heads, HEAD_DIM, HEAD_DIM), F32)],
        scratch_shapes=[pltpu.VMEM((n_heads, HEAD_DIM, HEAD_DIM), F32)],
        compiler_params=_params(("parallel", "arbitrary")),
        name="dn_chunks",
    )(qn, kn, vv, gb, og, norm_w)


def _dn_step_kernel(x_ref, cs_ref, bg_ref, og_ref, s_ref, cw_ref, al_ref, dtb_ref, nw_ref,
                    y_ref, s_out_ref, cs_out_ref):
    width3 = x_ref.shape[2]
    width = width3 // 3
    n_heads = width // HEAD_DIM
    xn = x_ref[0]
    cs = cs_ref[0]
    y = xn * cw_ref[CONV_W - 1:CONV_W, :]
    for j in range(CONV_W - 1):
        y = y + cs[j:j + 1, :] * cw_ref[j:j + 1, :]
    for j in range(1, CONV_W - 1):
        cs_out_ref[0, j - 1:j, :] = cs_ref[0, j:j + 1, :]
    cs_out_ref[0, CONV_W - 2:CONV_W - 1, :] = xn
    a = _silu(y)
    qn = _head_l2norm(a[:, :width], n_heads, HEAD_DIM ** -0.5)
    kn = _head_l2norm(a[:, width:2 * width], n_heads, 1.0)
    vv = a[:, 2 * width:]
    gbv = _conv_gate_math(bg_ref[0], al_ref[...], dtb_ref[...])
    og = og_ref[0]
    rows = 16
    rowi = lax.broadcasted_iota(I32, (rows, HEAD_DIM), 0)
    bround = lambda t: t.astype(BF16).astype(F32)
    for h in range(n_heads):
        sl = slice(h * HEAD_DIM, (h + 1) * HEAD_DIM)
        q, k, v = qn[:, sl], kn[:, sl], vv[:, sl]
        beta = gbv[:, h:h + 1]
        eg = jnp.exp(gbv[:, 8 + h:9 + h])
        s = s_ref[0, h]
        lhs = jnp.where(rowi == 0, k * beta * eg, jnp.where(rowi == 1, q * eg, 0.0))
        prod = jnp.dot(lhs.astype(BF16), s.astype(BF16), preferred_element_type=F32)
        v_new = v * beta - prod[0:1, :]
        attn = jnp.sum(bround(q) * bround(k), axis=1, keepdims=True)
        o = prod[1:2, :] + bround(attn) * bround(v_new)
        outer = _tn_dot(jnp.where(rowi == 0, k, 0.0).astype(BF16),
                        jnp.broadcast_to(v_new, (rows, HEAD_DIM)).astype(BF16))
        s_out_ref[0, h] = s * eg + outer
        o = o * lax.rsqrt(jnp.mean(o * o, axis=1, keepdims=True) + NORM_EPS) * nw_ref[...]
        y_ref[0, :, sl] = o * _silu(og[:, sl])


def _dn_step(x3, conv_state, bg3, og3, dn_state, conv_w, al_row, dtb_row, norm_w):
    n_seq, _, width3 = x3.shape
    width = width3 // 3
    n_heads = width // HEAD_DIM
    vec = lambda n: pl.BlockSpec((1, 1, n), lambda s: (s, 0, 0))
    const = lambda r, n: pl.BlockSpec((r, n), lambda s: (0, 0))
    return pl.pallas_call(
        _dn_step_kernel,
        grid=(n_seq,),
        in_specs=[vec(width3),
                  pl.BlockSpec((1, CONV_W - 1, width3), lambda s: (s, 0, 0)),
                  vec(HEAD_DIM), vec(width),
                  pl.BlockSpec((1, n_heads, HEAD_DIM, HEAD_DIM), lambda s: (s, 0, 0, 0)),
                  const(CONV_W, width3), const(1, HEAD_DIM), const(1, HEAD_DIM), const(1, HEAD_DIM)],
        out_specs=[vec(width),
                   pl.BlockSpec((1, n_heads, HEAD_DIM, HEAD_DIM), lambda s: (s, 0, 0, 0)),
                   pl.BlockSpec((1, CONV_W - 1, width3), lambda s: (s, 0, 0))],
        out_shape=[jax.ShapeDtypeStruct((n_seq, 1, width), F32),
                   jax.ShapeDtypeStruct(dn_state.shape, F32),
                   jax.ShapeDtypeStruct(conv_state.shape, F32)],
        compiler_params=_params(("parallel",)),
        name="dn_step",
    )(x3, conv_state, bg3, og3, dn_state, conv_w, al_row, dtb_row, norm_w)


def _top2_sum(a, b, c, d):
    hi1, lo1 = jnp.maximum(a, b), jnp.minimum(a, b)
    hi2, lo2 = jnp.maximum(c, d), jnp.minimum(c, d)
    return jnp.maximum(hi1, hi2) + jnp.maximum(jnp.minimum(hi1, hi2), jnp.maximum(lo1, lo2))


def _outproj_router_kernel(ya_ref, yb_ref, ys_ref, x_ref, w_ref, g_ref, b_ref, wr_ref, rb_ref,
                           x1_ref, info_ref, infot_ref, cnt_ref, carry_ref, mix_ref, *, alpha, n_valid):
    i = pl.program_id(0)
    last = pl.num_programs(0) - 1
    tm = x_ref.shape[0]
    att_w = ya_ref.shape[1]
    dot = functools.partial(jnp.dot, preferred_element_type=F32)

    @pl.when(i == 0)
    def _():
        carry_ref[...] = jnp.zeros(carry_ref.shape, F32)

    @pl.when(i < last)
    def _():
        mix_ref[...] = (dot(ya_ref[...], w_ref[0, :att_w, :].astype(BF16))
                        + dot(yb_ref[...], w_ref[0, att_w:, :].astype(BF16)))

    @pl.when(i == last)
    def _():
        mix_ref[...] = dot(ys_ref[...], w_ref[0].astype(BF16))

    x1 = _layer_norm(alpha * x_ref[...] + mix_ref[...], g_ref[...], b_ref[...])
    x1_ref[...] = x1

    scores = _sigmoid(_nt_dot(wr_ref[...], x1.astype(BF16)))
    selv = scores + rb_ref[:, 0:1]
    sel = [selv[e:e + 1, :] for e in range(N_EXPERTS)]
    sc = [scores[e:e + 1, :] for e in range(N_EXPERTS)]
    n_groups = N_EXPERTS // EXPERTS_PER_GROUP
    gs = [_top2_sum(*sel[EXPERTS_PER_GROUP * g:EXPERTS_PER_GROUP * (g + 1)]) for g in range(n_groups)]
    best, bidx = gs[0], jnp.zeros((1, tm), I32)
    for g in range(1, n_groups):
        better = gs[g] > best
        best = jnp.where(better, gs[g], best)
        bidx = jnp.where(better, g, bidx)
    ninf = jnp.full((1, tm), -jnp.inf, F32)
    vals = [jnp.where(bidx == e // EXPERTS_PER_GROUP, sel[e], ninf) for e in range(N_EXPERTS)]
    b1, i1, s1 = vals[0], jnp.zeros((1, tm), I32), sc[0]
    for e in range(1, N_EXPERTS):
        better = vals[e] > b1
        b1 = jnp.where(better, vals[e], b1)
        i1 = jnp.where(better, e, i1)
        s1 = jnp.where(better, sc[e], s1)
    b2, i2, s2 = ninf, jnp.zeros((1, tm), I32), jnp.zeros((1, tm), F32)
    for e in range(N_EXPERTS):
        cand = jnp.where(i1 == e, ninf, vals[e])
        better = cand > b2
        b2 = jnp.where(better, cand, b2)
        i2 = jnp.where(better, e, i2)
        s2 = jnp.where(better, sc[e], s2)
    tok = i * tm + lax.broadcasted_iota(I32, (1, tm), 1)
    valid = tok < n_valid
    inv = 1.0 / (s1 + s2)
    g1 = jnp.where(valid, s1 * inv, 0.0)
    g2 = jnp.where(valid, s2 * inv, 0.0)

    erow = lax.broadcasted_iota(I32, (N_EXPERTS, tm), 0)
    oh1 = jnp.where((erow == i1) & valid, 1.0, 0.0)
    oh2 = jnp.where((erow == i2) & valid, 1.0, 0.0)
    both = oh1 + oh2
    upper = jnp.where(lax.broadcasted_iota(I32, (tm, tm), 0) <= lax.broadcasted_iota(I32, (tm, tm), 1), 1.0, 0.0)
    incl = jnp.dot(both.astype(BF16), upper.astype(BF16), preferred_element_type=F32)
    before = carry_ref[:, 0:1] + incl - both
    r1 = jnp.sum(oh1 * before, axis=0, keepdims=True)
    r2 = jnp.sum(oh2 * before, axis=0, keepdims=True)
    carry_ref[...] = carry_ref[...] + incl[:, tm - 1:tm]
    cnt_ref[...] = carry_ref[...]

    row8 = lax.broadcasted_iota(I32, (8, tm), 0)
    info = jnp.zeros((8, tm), F32)
    for r, val in enumerate((i1.astype(F32), i2.astype(F32), g1, g2, r1, r2)):
        info = jnp.where(row8 == r, val, info)
    info_ref[...] = info
    infot_ref[...] = jnp.concatenate([info, jnp.zeros((HEAD_DIM - 8, tm), F32)], axis=0).T


def _outproj_router(ya, yb, ys, x, w_out, layer, ln_g, ln_b, wr_t, rb_col, alpha, n_valid):
    nt, d = x.shape
    dm = w_out.shape[1]
    prompt_tiles = ya.shape[0] // ROW_TILE
    assert nt == (prompt_tiles + 1) * ROW_TILE and ys.shape == (ROW_TILE, dm)
    prompt_spec = lambda n: pl.BlockSpec((ROW_TILE, n), lambda i: (jnp.minimum(i, prompt_tiles - 1), 0))
    kern = functools.partial(_outproj_router_kernel, alpha=alpha, n_valid=n_valid)
    return pl.pallas_call(
        kern,
        grid=(nt // ROW_TILE,),
        in_specs=[prompt_spec(ya.shape[1]), prompt_spec(yb.shape[1]),
                  pl.BlockSpec((ROW_TILE, dm), lambda i: (0, 0)),
                  pl.BlockSpec((ROW_TILE, d), lambda i: (i, 0)),
                  pl.BlockSpec((1, dm, d), lambda i: (layer, 0, 0), pipeline_mode=pl.Buffered(1)),
                  pl.BlockSpec((1, d), lambda i: (0, 0)),
                  pl.BlockSpec((1, d), lambda i: (0, 0)),
                  pl.BlockSpec((N_EXPERTS, d), lambda i: (0, 0)),
                  pl.BlockSpec((N_EXPERTS, HEAD_DIM), lambda i: (0, 0))],
        out_specs=[pl.BlockSpec((ROW_TILE, d), lambda i: (i, 0)),
                   pl.BlockSpec((8, ROW_TILE), lambda i: (0, i)),
                   pl.BlockSpec((ROW_TILE, HEAD_DIM), lambda i: (i, 0)),
                   pl.BlockSpec((N_EXPERTS, HEAD_DIM), lambda i: (0, 0))],
        out_shape=[jax.ShapeDtypeStruct((nt, d), F32),
                   jax.ShapeDtypeStruct((8, nt), F32),
                   jax.ShapeDtypeStruct((nt, HEAD_DIM), F32),
                   jax.ShapeDtypeStruct((N_EXPERTS, HEAD_DIM), F32)],
        scratch_shapes=[pltpu.VMEM((N_EXPERTS, HEAD_DIM), F32), pltpu.VMEM((ROW_TILE, d), F32)],
        compiler_params=_params(("arbitrary",)),
        name="outproj_router",
    )(ya, yb, ys, x, w_out, ln_g, ln_b, wr_t, rb_col)


def _row_copy(src_hbm, src_row, dst_ref, dst_row, sem):
    return pltpu.make_async_copy(src_hbm.at[pl.ds(src_row, 1), :], dst_ref.at[pl.ds(dst_row, 1), :], sem)


def _moe_expert_kernel(rt_ref, be_ref, nu_ref, x_hbm, wg_ref, wu_ref, wd_ref, o_ref, xbuf, sems):
    m = pl.program_id(0)
    n_used = nu_ref[0]
    tm = o_ref.shape[0]

    def issue(block, slot):
        def start(r, carry):
            _row_copy(x_hbm, rt_ref[block * tm + r], xbuf.at[slot], r, sems.at[slot]).start()
            return carry

        lax.fori_loop(0, tm, start, 0, unroll=ROW_DMA_UNROLL)

    def drain(slot):
        pltpu.make_async_copy(x_hbm.at[pl.ds(0, tm), :], xbuf.at[slot], sems.at[slot]).wait()

    @pl.when(m == 0)
    def _():
        issue(0, 0)

    @pl.when(m + 1 < n_used)
    def _():
        issue(m + 1, (m + 1) % 2)

    @pl.when(m < n_used)
    def _():
        drain(m % 2)
        xb = xbuf[m % 2].astype(BF16)
        hid = _silu(jnp.dot(xb, wg_ref[0].astype(BF16), preferred_element_type=F32)) * jnp.dot(
            xb, wu_ref[0].astype(BF16), preferred_element_type=F32)
        o_ref[...] = jnp.dot(hid.astype(BF16), wd_ref[0].astype(BF16), preferred_element_type=F32)

    @pl.when(m >= n_used)
    def _():
        o_ref[...] = jnp.zeros(o_ref.shape, o_ref.dtype)


def _moe_experts(row_tok, blk_e, n_used, x, wg, wu, wd):
    d = x.shape[1]
    n_rows = row_tok.shape[0]
    f = wg.shape[2]
    return pl.pallas_call(
        _moe_expert_kernel,
        grid_spec=pltpu.PrefetchScalarGridSpec(
            num_scalar_prefetch=3,
            grid=(n_rows // MOE_TILE,),
            in_specs=[pl.BlockSpec(memory_space=pl.ANY),
                      pl.BlockSpec((1, d, f), lambda m, rt, be, nu: (be[m], 0, 0)),
                      pl.BlockSpec((1, d, f), lambda m, rt, be, nu: (be[m], 0, 0)),
                      pl.BlockSpec((1, f, d), lambda m, rt, be, nu: (be[m], 0, 0), pipeline_mode=pl.Buffered(1))],
            out_specs=pl.BlockSpec((MOE_TILE, d), lambda m, rt, be, nu: (m, 0)),
            scratch_shapes=[pltpu.VMEM((2, MOE_TILE, d), F32), pltpu.SemaphoreType.DMA((2,))]),
        out_shape=jax.ShapeDtypeStruct((n_rows, d), F32),
        compiler_params=_params(("arbitrary",)),
        name="moe_experts",
    )(row_tok, blk_e, n_used, x, wg, wu, wd)


def _moe_combine_kernel(dest_ref, y_hbm, x_ref, it_ref, g_ref, b_ref, o_ref, buf_ref, sems, *, alpha, nt):
    i = pl.program_id(0)
    tm = x_ref.shape[0]

    def issue(tile, slot):
        def start(r, carry):
            _row_copy(y_hbm, dest_ref[tile * tm + r], buf_ref.at[slot, 0], r, sems.at[slot]).start()
            _row_copy(y_hbm, dest_ref[nt + tile * tm + r], buf_ref.at[slot, 1], r, sems.at[slot]).start()
            return carry

        lax.fori_loop(0, tm, start, 0, unroll=ROW_DMA_UNROLL)

    def drain(slot):
        for half in range(2):
            pltpu.make_async_copy(y_hbm.at[pl.ds(0, tm), :], buf_ref.at[slot, half], sems.at[slot]).wait()

    @pl.when(i == 0)
    def _():
        issue(0, 0)

    @pl.when(i + 1 < pl.num_programs(0))
    def _():
        issue(i + 1, (i + 1) % 2)

    slot = i % 2
    drain(slot)
    it = it_ref[...]
    f = buf_ref[slot, 0] * it[:, 2:3] + buf_ref[slot, 1] * it[:, 3:4]
    o_ref[...] = _layer_norm(alpha * x_ref[...] + f, g_ref[...], b_ref[...])


def _moe_combine(dest, y_sorted, x1, info_t, ln_g, ln_b, alpha):
    nt, d = x1.shape
    kern = functools.partial(_moe_combine_kernel, alpha=alpha, nt=nt)
    return pl.pallas_call(
        kern,
        grid_spec=pltpu.PrefetchScalarGridSpec(
            num_scalar_prefetch=1,
            grid=(nt // ROW_TILE,),
            in_specs=[pl.BlockSpec(memory_space=pl.ANY),
                      pl.BlockSpec((ROW_TILE, d), lambda i, ds: (i, 0)),
                      pl.BlockSpec((ROW_TILE, HEAD_DIM), lambda i, ds: (i, 0)),
                      pl.BlockSpec((1, d), lambda i, ds: (0, 0)),
                      pl.BlockSpec((1, d), lambda i, ds: (0, 0))],
            out_specs=pl.BlockSpec((ROW_TILE, d), lambda i, ds: (i, 0)),
            scratch_shapes=[pltpu.VMEM((2, 2, ROW_TILE, d), F32), pltpu.SemaphoreType.DMA((2,))]),
        out_shape=jax.ShapeDtypeStruct((nt, d), F32),
        compiler_params=_params(("arbitrary",)),
        name="moe_combine",
    )(dest, y_sorted, x1, info_t, ln_g, ln_b)


def _moe_layer(ya, yb, ys, x, w_out, layer, ln1_g, ln1_b, wr_t, rb_col, wg, wu, wd, ln2_g, ln2_b, alpha, n_valid):
    nt, d = x.shape
    first_expert = layer * N_EXPERTS
    x1, info, info_t, cnt = _outproj_router(ya, yb, ys, x, w_out, layer, ln1_g, ln1_b, wr_t, rb_col, alpha, n_valid)
    n_blocks = -(-(2 * n_valid) // MOE_TILE) + N_EXPERTS
    counts = cnt[:, 0].astype(I32)
    padded = (counts + MOE_TILE - 1) // MOE_TILE * MOE_TILE
    pad_end = jnp.cumsum(padded)
    pad_start = pad_end - padded
    e1, e2 = info[0].astype(I32), info[1].astype(I32)
    dest1 = pad_start[e1] + info[4].astype(I32)
    dest2 = pad_start[e2] + info[5].astype(I32)
    tok = jnp.arange(n_valid, dtype=I32)
    row_tok = jnp.zeros((n_blocks * MOE_TILE,), I32).at[dest1[:n_valid]].set(tok).at[dest2[:n_valid]].set(tok)
    n_used = (pad_end[-1] // MOE_TILE).astype(I32)
    blk_start = jnp.arange(n_blocks, dtype=I32) * MOE_TILE
    blk_e = jnp.minimum(jnp.sum(pad_end[None, :] <= blk_start[:, None], axis=1), N_EXPERTS - 1).astype(I32)
    blk_e = jnp.where(jnp.arange(n_blocks) < n_used, blk_e, blk_e[jnp.maximum(n_used - 1, 0)])
    valid = jnp.arange(nt) < n_valid
    dest = jnp.concatenate([jnp.where(valid, dest1, 0), jnp.where(valid, dest2, 0)]).astype(I32)
    nu = n_used.reshape(1)
    ys = _moe_experts(row_tok, blk_e + first_expert, nu, x1, wg, wu, wd)
    return _moe_combine(dest, ys, x1, info_t, ln2_g, ln2_b, alpha)


def kernel(x_prompt, x_sample, cache_k, cache_v, state_dn, state_conv, page_table, ln_in_g, ln_in_b, rel_bias,
           w_router, router_bias, w_in, w_out, conv_w, a_log, dt_bias, dn_norm_w, ln1_g, ln1_b, ln2_g, ln2_b,
           w_gate, w_up, w_down):
    batch, seq, d = x_prompt.shape
    n_seq, dec_seq, _ = x_sample.shape
    depth = w_in.shape[0]
    n_pool, page = cache_k.shape[1], cache_k.shape[2]
    kvh = cache_k.shape[3]
    n_pages = page_table.shape[1]
    past_len = n_pages * page
    assert dec_seq == 1 and past_len % MOBA_BLOCK == 0 and n_seq <= ROW_TILE
    att_w = d // 2
    dn_w = d - att_w
    heads = att_w // HEAD_DIM
    dn_heads = dn_w // HEAD_DIM
    kv_w = kvh * HEAD_DIM
    assert dn_heads == 8 and heads == ATT_GROUP * kvh
    alpha = (2 * depth) ** 0.25
    n_prompt = batch * seq
    nt = n_prompt + ROW_TILE
    n_valid = n_prompt + n_seq

    x = jnp.concatenate([x_prompt.reshape(n_prompt, d), x_sample.reshape(n_seq, d),
                         jnp.zeros((ROW_TILE - n_seq, d), F32)], axis=0)
    x = _input_layer_norm(x, ln_in_g, ln_in_b)

    widths = (att_w, kv_w, kv_w, 3 * dn_w, dn_w, HEAD_DIM)
    in_cols = w_in.shape[2]
    w_in_b = jnp.pad(w_in, ((0, 0), (0, 0), (0, sum(widths) - in_cols))).astype(BF16)
    all_experts = lambda w: w.astype(F32).reshape((depth * N_EXPERTS,) + w.shape[2:])
    wg_all, wu_all, wd_all = all_experts(w_gate), all_experts(w_up), all_experts(w_down)
    wr_t = w_router.T.astype(BF16)
    rb_col = jnp.broadcast_to(router_bias.astype(F32)[:, None], (N_EXPERTS, HEAD_DIM))
    lane_pad = lambda v: jnp.pad(v.astype(F32), ((0, 0), (dn_heads, HEAD_DIM - 2 * dn_heads)))
    al_rows, dtb_rows = lane_pad(a_log), lane_pad(dt_bias)
    bias_tiles = _bias_tiles(rel_bias, seq // MOBA_BLOCK)
    pt = page_table.astype(I32)

    ks_p, vs_p, dn_p, cv_p, ks_s, vs_s, dn_s, cv_s = [], [], [], [], [], [], [], []
    for l in range(depth):
        qa, ka, va, qkvb, og, bg = _in_projection(x, w_in_b, l, widths)
        al_row, dtb_row = al_rows[l:l + 1], dtb_rows[l:l + 1]
        norm_w = dn_norm_w[l].reshape(1, HEAD_DIM).astype(F32)

        ya_p = _moba_prompt(qa, ka, va, bias_tiles, batch, seq)
        qn, kn, vv, gb = _dn_prep(qkvb, bg, conv_w[l], al_row, dtb_row, batch, seq)
        yb_p, s_p = _dn_chunks(qn, kn, vv, gb, og, norm_w, batch, seq)

        sl = slice(n_prompt, n_prompt + n_seq)
        q3 = qa[sl].reshape(n_seq, heads, HEAD_DIM)
        kn3 = ka[sl].reshape(n_seq, kvh, HEAD_DIM)
        vn3 = va[sl].reshape(n_seq, kvh, HEAD_DIM)
        ksum = _past_block_sums(cache_k, l, pt)
        top = _decode_select(q3, ksum.transpose(0, 2, 1, 3))[:, :, :MOBA_TOPK]
        ya_s = _decode_attend(q3, kn3, vn3, cache_k, cache_v, l, pt, top, rel_bias, past_len)
        yb_s, s_s, c_s = _dn_step(qkvb[sl].reshape(n_seq, 1, 3 * dn_w), state_conv[l],
                                  bg[sl].reshape(n_seq, 1, HEAD_DIM), og[sl].reshape(n_seq, 1, dn_w),
                                  state_dn[l], conv_w[l], al_row, dtb_row, norm_w)

        y_s = jnp.concatenate([ya_s.reshape(n_seq, att_w), yb_s.reshape(n_seq, dn_w)], axis=1).astype(BF16)
        y_s = jnp.pad(y_s, ((0, ROW_TILE - n_seq), (0, 0)))
        x = _moe_layer(ya_p, yb_p, y_s, x, w_out.astype(F32), l, ln1_g[l].reshape(1, d), ln1_b[l].reshape(1, d),
                       wr_t, rb_col, wg_all, wu_all, wd_all, ln2_g[l].reshape(1, d), ln2_b[l].reshape(1, d),
                       alpha, n_valid)

        ks_p.append(ka)
        vs_p.append(va)
        dn_p.append(s_p)
        cv_p.append(qkvb[:n_prompt].reshape(batch, seq, 3 * dn_w)[:, seq - (CONV_W - 1):])
        ks_s.append(kn3.reshape(n_seq, 1, kvh, HEAD_DIM))
        vs_s.append(vn3.reshape(n_seq, 1, kvh, HEAD_DIM))
        dn_s.append(s_s)
        cv_s.append(c_s)

    y_prompt = x[:n_prompt].reshape(batch, seq, d)
    y_sample = x[n_prompt:n_prompt + n_seq].reshape(n_seq, 1, d)
    prompt_kv = lambda rows: jnp.stack([r[:n_prompt] for r in rows]).reshape(depth, batch, seq, kvh, HEAD_DIM)
    return (y_prompt, y_sample, prompt_kv(ks_p), prompt_kv(vs_p), jnp.stack(dn_p), jnp.stack(cv_p),
            jnp.stack(ks_s), jnp.stack(vs_s), jnp.stack(dn_s), jnp.stack(cv_s))
```

```python
import functools
import math

import jax
import jax.numpy as jnp
from jax import lax
from jax.experimental import pallas as pl
from jax.experimental.pallas import tpu as pltpu

F32 = jnp.float32
BF16 = jnp.bfloat16
I32 = jnp.int32
HIGHEST = lax.Precision.HIGHEST

HEAD_DIM = 128
ATT_GROUP = 2
MOBA_BLOCK = 256
MOBA_TOPK = 3
MOBA_Q_SUB = 256
REL_BUCKETS = 32
REL_MAX_DIST = 2048
DN_CHUNK = 64
CONV_W = 4
N_EXPERTS = 16
EXPERTS_PER_GROUP = 4
LN_EPS = 1e-5
NORM_EPS = 1e-6
ROW_TILE = 256
MOE_TILE = 256
ROW_DMA_UNROLL = 8
PAGES_PER_STEP = 16
NEG = -1e30
VMEM_LIMIT_V7X = 56 * 1024 * 1024


def _params(semantics):
    return pltpu.CompilerParams(dimension_semantics=semantics, vmem_limit_bytes=VMEM_LIMIT_V7X)


def _nt_dot(a, b, precision=None):
    return lax.dot_general(a, b, (((1,), (1,)), ((), ())), precision=precision, preferred_element_type=F32)


def _tn_dot(a, b, precision=None):
    return lax.dot_general(a, b, (((0,), (0,)), ((), ())), precision=precision, preferred_element_type=F32)


def _hdot(a, b):
    return jnp.dot(a, b, precision=HIGHEST, preferred_element_type=F32)


def _bdot(a, b):
    return jnp.dot(a.astype(BF16), b.astype(BF16), preferred_element_type=F32)


def _dot3(a, b):
    ah, bh = a.astype(BF16), b.astype(BF16)
    al, bl = (a - ah.astype(F32)).astype(BF16), (b - bh.astype(F32)).astype(BF16)
    dot = functools.partial(jnp.dot, preferred_element_type=F32)
    return dot(ah, bh) + (dot(ah, bl) + dot(al, bh))


def _layer_norm(x, g, b):
    mu = jnp.mean(x, axis=-1, keepdims=True)
    xc = x - mu
    var = jnp.mean(xc * xc, axis=-1, keepdims=True)
    return xc * lax.rsqrt(var + LN_EPS) * g + b


def _silu(x):
    return x * (1.0 / (1.0 + jnp.exp(-x)))


def _sigmoid(x):
    return 1.0 / (1.0 + jnp.exp(-x))


def _softplus(x):
    return jnp.maximum(x, 0.0) + jnp.log(1.0 + jnp.exp(-jnp.abs(x)))


def _rel_bucket(dist):
    n = jnp.maximum(dist, 0)
    max_exact = REL_BUCKETS // 2
    nf = jnp.maximum(n, 1).astype(F32)
    large = max_exact + (jnp.log(nf / max_exact) / math.log(REL_MAX_DIST / max_exact)
                         * (REL_BUCKETS - max_exact)).astype(I32)
    large = jnp.minimum(large, REL_BUCKETS - 1)
    return jnp.where(n < max_exact, n, large)


def _ln_kernel(x_ref, g_ref, b_ref, o_ref):
    o_ref[...] = _layer_norm(x_ref[...], g_ref[...], b_ref[...])


def _input_layer_norm(x, g, b):
    nt, d = x.shape
    return pl.pallas_call(
        _ln_kernel,
        grid=(nt // ROW_TILE,),
        in_specs=[pl.BlockSpec((ROW_TILE, d), lambda i: (i, 0)),
                  pl.BlockSpec((1, d), lambda i: (0, 0)),
                  pl.BlockSpec((1, d), lambda i: (0, 0))],
        out_specs=pl.BlockSpec((ROW_TILE, d), lambda i: (i, 0)),
        out_shape=jax.ShapeDtypeStruct((nt, d), F32),
        compiler_params=_params(("parallel",)),
        name="input_layer_norm",
    )(x, g.reshape(1, d), b.reshape(1, d))


def _inproj_kernel(x_ref, w_ref, *o_refs):
    xb = x_ref[...].astype(BF16)
    off = 0
    for o_ref in o_refs:
        n = o_ref.shape[1]
        o_ref[...] = jnp.dot(xb, w_ref[0, :, off:off + n], preferred_element_type=F32)
        off += n


def _in_projection(x, w_bf16, layer, widths):
    nt, d = x.shape
    cols = w_bf16.shape[2]
    assert sum(widths) == cols
    return pl.pallas_call(
        _inproj_kernel,
        grid=(nt // ROW_TILE,),
        in_specs=[pl.BlockSpec((ROW_TILE, d), lambda i: (i, 0)),
                  pl.BlockSpec((1, d, cols), lambda i: (layer, 0, 0), pipeline_mode=pl.Buffered(1))],
        out_specs=[pl.BlockSpec((ROW_TILE, n), lambda i: (i, 0)) for n in widths],
        out_shape=[jax.ShapeDtypeStruct((nt, n), F32) for n in widths],
        compiler_params=_params(("parallel",)),
        name="in_projection",
    )(x, w_bf16)


def _bias_tile_kernel(tab_ref, o_ref):
    h = pl.program_id(0)
    d = pl.program_id(1)
    key = lax.broadcasted_iota(I32, (MOBA_BLOCK, MOBA_BLOCK), 0)
    qry = lax.broadcasted_iota(I32, (MOBA_BLOCK, MOBA_BLOCK), 1)
    dist = d * MOBA_BLOCK + qry - key
    bucket = _rel_bucket(dist)
    acc = jnp.zeros((MOBA_BLOCK, MOBA_BLOCK), F32)
    for k in range(REL_BUCKETS):
        acc = jnp.where(bucket == k, tab_ref[k, h], acc)
    o_ref[0, 0] = jnp.where(dist >= 0, acc, NEG)


def _bias_tiles(rel_bias, n_blk):
    heads = rel_bias.shape[1]
    return pl.pallas_call(
        _bias_tile_kernel,
        grid=(heads, n_blk),
        in_specs=[pl.BlockSpec(memory_space=pltpu.SMEM)],
        out_specs=pl.BlockSpec((1, 1, MOBA_BLOCK, MOBA_BLOCK), lambda h, d: (h, d, 0, 0)),
        out_shape=jax.ShapeDtypeStruct((heads, n_blk, MOBA_BLOCK, MOBA_BLOCK), F32),
        compiler_params=_params(("parallel", "parallel")),
        name="bias_tiles",
    )(rel_bias.astype(F32))


def _moba_prompt_kernel(q_ref, k_ref, v_ref, bias_ref, o_ref, kaug_ref, vt_ref, kmean_ref):
    i = pl.program_id(2)
    seq = k_ref.shape[0]
    n_blk = seq // MOBA_BLOCK
    scale = HEAD_DIM ** -0.5

    @pl.when(i == 0)
    def _():
        kaug_ref[:, :HEAD_DIM] = k_ref[...].astype(BF16)
        rows = lax.broadcasted_iota(I32, (seq, HEAD_DIM), 0) // MOBA_BLOCK
        lanes = lax.broadcasted_iota(I32, (seq, HEAD_DIM), 1)
        kaug_ref[:, HEAD_DIM:] = jnp.where(rows == lanes, 1.0, 0.0).astype(BF16)
        for j in range(n_blk):
            blk = slice(j * MOBA_BLOCK, (j + 1) * MOBA_BLOCK)
            vt_ref[j] = v_ref[blk, :].T.astype(BF16)
            kmean_ref[j:j + 1, :] = jnp.mean(k_ref[blk, :], axis=0, keepdims=True)

    tq = q_ref.shape[0]
    kmean = kmean_ref[...].astype(BF16)
    chains = []
    for hh in range(ATT_GROUP):
        q = q_ref[:, hh * HEAD_DIM:(hh + 1) * HEAD_DIM]
        gate = _nt_dot(kmean, q.astype(BF16))
        jidx = lax.broadcasted_iota(I32, gate.shape, 0)
        rank = jnp.zeros(gate.shape, I32)
        for jp in range(n_blk):
            row = gate[jp:jp + 1, :]
            before = (row > gate) | ((row == gate) & (jp < jidx))
            rank = rank + jnp.where(before, jnp.where(jp < i, 1, 0), 0)
        sel = ((jidx < i) & (rank < MOBA_TOPK)) | (jidx == i)
        selneg = jnp.where(sel, 0.0, NEG)
        selneg = jnp.concatenate([selneg, jnp.zeros((HEAD_DIM - n_blk, tq), F32)], axis=0)
        q_aug_t = jnp.concatenate([q.T.astype(BF16), selneg.astype(BF16)], axis=0)
        for c0 in range(0, tq, MOBA_Q_SUB):
            chains.append((hh, c0, q_aug_t[:, c0:c0 + MOBA_Q_SUB]))

    dot = functools.partial(jnp.dot, preferred_element_type=F32)

    def logits(t):
        j = jnp.maximum(i - t, 0)
        tb = jnp.minimum(t, n_blk - 1)
        kj = kaug_ref[pl.ds(pl.multiple_of(j * MOBA_BLOCK, MOBA_BLOCK), MOBA_BLOCK), :]
        return tuple(dot(kj, qc) * scale + bias_ref[hh, tb, :, c0:c0 + MOBA_Q_SUB] for hh, c0, qc in chains)

    def body(t, carry):
        ss, stats = carry
        ss_next = logits(t + 1)
        vtj = vt_ref[i - t]
        m_new = [jnp.maximum(c[0], jnp.max(s, axis=0, keepdims=True)) for c, s in zip(stats, ss)]
        ps = [jnp.exp(s - mn) for s, mn in zip(ss, m_new)]
        alphas = [jnp.exp(c[0] - mn) for c, mn in zip(stats, m_new)]
        pvs = [dot(vtj, p.astype(BF16)) for p in ps]
        ls = [a * c[1] + jnp.sum(p, axis=0, keepdims=True) for a, c, p in zip(alphas, stats, ps)]
        accs = [a * c[2] + pv for a, c, pv in zip(alphas, stats, pvs)]
        return ss_next, tuple(zip(m_new, ls, accs))

    init = tuple((jnp.full((1, MOBA_Q_SUB), -jnp.inf, F32), jnp.zeros((1, MOBA_Q_SUB), F32),
                  jnp.zeros((HEAD_DIM, MOBA_Q_SUB), F32)) for _ in chains)
    _, final = lax.fori_loop(0, i + 1, body, (logits(0), init))
    for (hh, c0, _), (_, l, acc) in zip(chains, final):
        o_ref[c0:c0 + MOBA_Q_SUB, hh * HEAD_DIM:(hh + 1) * HEAD_DIM] = (acc / l).T.astype(o_ref.dtype)


def _moba_prompt(qa, ka, va, bias_tiles, batch, seq):
    kvh = ka.shape[1] // HEAD_DIM
    group_w = ATT_GROUP * HEAD_DIM
    n_blk = seq // MOBA_BLOCK
    assert seq % MOBA_BLOCK == 0 and n_blk <= HEAD_DIM
    return pl.pallas_call(
        _moba_prompt_kernel,
        grid=(kvh, batch, n_blk),
        in_specs=[pl.BlockSpec((MOBA_BLOCK, group_w), lambda g, b, i: (b * n_blk + i, g)),
                  pl.BlockSpec((seq, HEAD_DIM), lambda g, b, i: (b, g)),
                  pl.BlockSpec((seq, HEAD_DIM), lambda g, b, i: (b, g)),
                  pl.BlockSpec((ATT_GROUP, n_blk, MOBA_BLOCK, MOBA_BLOCK), lambda g, b, i: (g, 0, 0, 0))],
        out_specs=pl.BlockSpec((MOBA_BLOCK, group_w), lambda g, b, i: (b * n_blk + i, g)),
        out_shape=jax.ShapeDtypeStruct((batch * seq, kvh * group_w), BF16),
        scratch_shapes=[pltpu.VMEM((seq, 2 * HEAD_DIM), BF16),
                        pltpu.VMEM((n_blk, HEAD_DIM, MOBA_BLOCK), BF16),
                        pltpu.VMEM((n_blk, HEAD_DIM), F32)],
        compiler_params=_params(("parallel", "parallel", "arbitrary")),
        name="moba_prompt",
    )(qa, ka, va, bias_tiles)


def _page_sum_kernel(pt_ref, *refs):
    o_ref = refs[-1]
    for u in range(0, PAGES_PER_STEP, 2):
        o_ref[0, u // 2] = jnp.sum(refs[u][0, 0], axis=0) + jnp.sum(refs[u + 1][0, 0], axis=0)


def _past_block_sums(cache, layer, page_table):
    n_seq, n_pages = page_table.shape
    page, kvh = cache.shape[2], cache.shape[3]
    assert 2 * page == MOBA_BLOCK and n_pages % PAGES_PER_STEP == 0
    steps = n_pages // PAGES_PER_STEP
    blk_per_step = PAGES_PER_STEP // 2

    def page_spec(u):
        return pl.BlockSpec((1, 1, page, kvh, HEAD_DIM),
                            lambda s, t, pt: (layer, pt[s, t * PAGES_PER_STEP + u], 0, 0, 0))

    return pl.pallas_call(
        _page_sum_kernel,
        grid_spec=pltpu.PrefetchScalarGridSpec(
            num_scalar_prefetch=1,
            grid=(n_seq, steps),
            in_specs=[page_spec(u) for u in range(PAGES_PER_STEP)],
            out_specs=pl.BlockSpec((1, blk_per_step, kvh, HEAD_DIM), lambda s, t, pt: (s, t, 0, 0))),
        out_shape=jax.ShapeDtypeStruct((n_seq, n_pages // 2, kvh, HEAD_DIM), F32),
        compiler_params=_params(("parallel", "arbitrary")),
        name="past_block_sums",
    )(page_table, *([cache] * PAGES_PER_STEP))


def _decode_select_kernel(q_ref, ksum_ref, o_ref):
    heads = q_ref.shape[1]
    n_blk = ksum_ref.shape[2]
    q = q_ref[0]
    qb = jnp.concatenate([q, q], axis=0).astype(BF16)
    hrow = lax.broadcasted_iota(I32, (heads, n_blk), 0)
    gate = jnp.zeros((heads, n_blk), F32)
    for g in range(heads // ATT_GROUP):
        kmean = ksum_ref[0, g] * (1.0 / MOBA_BLOCK)
        gate = jnp.where(hrow // ATT_GROUP == g, _nt_dot(qb, kmean.astype(BF16))[:heads], gate)
    nidx = lax.broadcasted_iota(I32, (heads, n_blk), 1)
    rank = jnp.zeros((heads, n_blk), I32)
    for n in range(n_blk):
        col = gate[:, n:n + 1]
        rank = rank + jnp.where((col > gate) | ((col == gate) & (n < nidx)), 1, 0)
    lane = lax.broadcasted_iota(I32, (heads, HEAD_DIM), 1)
    out = jnp.zeros((heads, HEAD_DIM), I32)
    for r in range(MOBA_TOPK):
        idx = jnp.sum(jnp.where(rank == r, nidx.astype(F32), 0.0), axis=1, keepdims=True)
        out = jnp.where(lane == r, idx.astype(I32), out)
    o_ref[0] = out


def _decode_select(q3, ksum):
    n_seq, heads, _ = q3.shape
    kvh, n_blk = ksum.shape[1], ksum.shape[2]
    return pl.pallas_call(
        _decode_select_kernel,
        grid=(n_seq,),
        in_specs=[pl.BlockSpec((1, heads, HEAD_DIM), lambda s: (s, 0, 0)),
                  pl.BlockSpec((1, kvh, n_blk, HEAD_DIM), lambda s: (s, 0, 0, 0))],
        out_specs=pl.BlockSpec((1, heads, HEAD_DIM), lambda s: (s, 0, 0)),
        out_shape=jax.ShapeDtypeStruct((n_seq, heads, HEAD_DIM), I32),
        compiler_params=_params(("parallel",)),
        name="decode_select",
    )(q3, ksum)


def _decode_attend_kernel(pt_ref, blk_ref, q_ref, kn_ref, vn_ref, tab_ref, ck_hbm, cv_hbm, o_ref,
                          kbuf, vbuf, sems, *, past_len, layer):
    n_tiles = 2 * MOBA_TOPK
    s = pl.program_id(0)
    heads = q_ref.shape[1]
    page = kbuf.shape[2]
    scale = HEAD_DIM ** -0.5
    rows = 16
    bround = lambda a: a.astype(BF16).astype(F32)
    lane = lax.broadcasted_iota(I32, (1, page), 1)

    def tile_copies(h):
        g = h // ATT_GROUP
        copies = []
        for u in range(n_tiles):
            pg = pt_ref[s, 2 * blk_ref[s * heads + h, u // 2] + (u % 2)]
            copies.append(pltpu.make_async_copy(ck_hbm.at[layer, pg, :, g, :], kbuf.at[h, u], sems.at[h]))
            copies.append(pltpu.make_async_copy(cv_hbm.at[layer, pg, :, g, :], vbuf.at[h, u], sems.at[h]))
        return copies

    for h in range(heads):
        for cp in tile_copies(h):
            cp.start()

    for h in range(heads):
        for cp in tile_copies(h):
            cp.wait()
        q = jnp.broadcast_to(q_ref[0, h:h + 1, :], (rows, HEAD_DIM)).astype(BF16)
        logits = []
        for u in range(n_tiles):
            kpos = blk_ref[s * heads + h, u // 2] * MOBA_BLOCK + (u % 2) * page + lane
            bucket = _rel_bucket(past_len - kpos)
            bias = jnp.zeros((1, page), F32)
            for k in range(REL_BUCKETS):
                bias = jnp.where(bucket == k, tab_ref[k, h], bias)
            logits.append(_nt_dot(q, kbuf[h, u].astype(BF16))[0:1, :] * scale + bias)
        g = h // ATT_GROUP
        kn = kn_ref[0, g:g + 1, :]
        vn = vn_ref[0, g:g + 1, :]
        own = jnp.sum(q[0:1, :].astype(F32) * bround(kn), axis=1, keepdims=True) * scale + tab_ref[0, h]

        m = own
        for lg in logits:
            m = jnp.maximum(m, jnp.max(lg, axis=1, keepdims=True))
        e_own = jnp.exp(own - m)
        es = [jnp.exp(lg - m) for lg in logits]
        den = e_own
        for e in es:
            den = den + jnp.sum(e, axis=1, keepdims=True)
        acc = bround(e_own / den) * bround(vn)
        for u in range(n_tiles):
            p = jnp.broadcast_to(es[u] / den, (rows, page)).astype(BF16)
            acc = acc + jnp.dot(p, vbuf[h, u].astype(BF16), preferred_element_type=F32)[0:1, :]
        o_ref[0, h:h + 1, :] = acc


def _decode_attend(q3, kn3, vn3, cache_k, cache_v, layer, page_table, blocks, rel_bias, past_len):
    n_seq, heads, _ = q3.shape
    kvh = kn3.shape[1]
    page = cache_k.shape[2]
    n_tiles = 2 * MOBA_TOPK
    return pl.pallas_call(
        functools.partial(_decode_attend_kernel, past_len=past_len, layer=layer),
        grid_spec=pltpu.PrefetchScalarGridSpec(
            num_scalar_prefetch=2,
            grid=(n_seq,),
            in_specs=[pl.BlockSpec((1, heads, HEAD_DIM), lambda s, pt, bk: (s, 0, 0)),
                      pl.BlockSpec((1, kvh, HEAD_DIM), lambda s, pt, bk: (s, 0, 0)),
                      pl.BlockSpec((1, kvh, HEAD_DIM), lambda s, pt, bk: (s, 0, 0)),
                      pl.BlockSpec(memory_space=pltpu.SMEM),
                      pl.BlockSpec(memory_space=pl.ANY),
                      pl.BlockSpec(memory_space=pl.ANY)],
            out_specs=pl.BlockSpec((1, heads, HEAD_DIM), lambda s, pt, bk: (s, 0, 0)),
            scratch_shapes=[pltpu.VMEM((heads, n_tiles, page, HEAD_DIM), F32),
                            pltpu.VMEM((heads, n_tiles, page, HEAD_DIM), F32),
                            pltpu.SemaphoreType.DMA((heads,))]),
        out_shape=jax.ShapeDtypeStruct((n_seq, heads, HEAD_DIM), F32),
        compiler_params=_params(("arbitrary",)),
        name="decode_attend",
    )(page_table, blocks.reshape(n_seq * heads, MOBA_TOPK), q3, kn3, vn3, rel_bias.astype(F32), cache_k, cache_v)


def _conv_gate_math(bg, al, dtb):
    lane = lax.broadcasted_iota(I32, bg.shape, 1)
    return jnp.where(lane < 8, _sigmoid(bg), -jnp.exp(al) * _softplus(bg + dtb))


def _head_l2norm(x, n_heads, mult):
    outs = []
    for h in range(n_heads):
        seg = x[:, h * HEAD_DIM:(h + 1) * HEAD_DIM]
        ss = jnp.sum(seg * seg, axis=1, keepdims=True)
        outs.append(seg * (lax.rsqrt(ss + NORM_EPS) * mult))
    return jnp.concatenate(outs, axis=1)


def _dn_prep_kernel(cur_ref, prev_ref, bg_ref, cw_ref, al_ref, dtb_ref, qn_ref, kn_ref, vv_ref, gb_ref):
    i = pl.program_id(1)
    t, width3 = cur_ref.shape
    width = width3 // 3
    n_heads = width // HEAD_DIM
    cur = cur_ref[...]
    prev = prev_ref[...] * jnp.where(i > 0, 1.0, 0.0)
    row8 = lax.broadcasted_iota(I32, (8, 1), 0)
    y = cur * cw_ref[CONV_W - 1:CONV_W, :]
    for k in range(1, CONV_W):
        rolled = pltpu.roll(cur, k, 0)
        head = jnp.where(row8 < k, pltpu.roll(prev, k, 0), rolled[:8])
        shifted = jnp.concatenate([head, rolled[8:]], axis=0)
        y = y + shifted * cw_ref[CONV_W - 1 - k:CONV_W - k, :]
    a = _silu(y)
    qn_ref[...] = _head_l2norm(a[:, :width], n_heads, HEAD_DIM ** -0.5)
    kn_ref[...] = _head_l2norm(a[:, width:2 * width], n_heads, 1.0)
    vv_ref[...] = a[:, 2 * width:]
    gb_ref[...] = _conv_gate_math(bg_ref[...], al_ref[...], dtb_ref[...])


def _dn_prep(qkvb, bg, conv_w, al_row, dtb_row, batch, seq):
    width3 = qkvb.shape[1]
    width = width3 // 3
    tiles = seq // ROW_TILE
    rows = batch * seq
    row_spec = lambda n: pl.BlockSpec((ROW_TILE, n), lambda b, i: (b * tiles + i, 0))
    return pl.pallas_call(
        _dn_prep_kernel,
        grid=(batch, tiles),
        in_specs=[row_spec(width3),
                  pl.BlockSpec((8, width3), lambda b, i: (jnp.maximum((b * tiles + i) * (ROW_TILE // 8) - 1, 0), 0)),
                  row_spec(HEAD_DIM),
                  pl.BlockSpec((CONV_W, width3), lambda b, i: (0, 0)),
                  pl.BlockSpec((1, HEAD_DIM), lambda b, i: (0, 0)),
                  pl.BlockSpec((1, HEAD_DIM), lambda b, i: (0, 0))],
        out_specs=[row_spec(width), row_spec(width), row_spec(width), row_spec(HEAD_DIM)],
        out_shape=[jax.ShapeDtypeStruct((rows, width), F32)] * 3 + [jax.ShapeDtypeStruct((rows, HEAD_DIM), F32)],
        compiler_params=_params(("parallel", "parallel")),
        name="dn_prep",
    )(qkvb, qkvb, bg, conv_w, al_row, dtb_row)


def _dn_chunk_kernel(qn_ref, kn_ref, vv_ref, gb_ref, og_ref, nw_ref, y_ref, s_out_ref, s_ref):
    c = pl.program_id(1)
    n_chunks = pl.num_programs(1)
    ch, width = qn_ref.shape
    n_heads = width // HEAD_DIM

    @pl.when(c == 0)
    def _():
        s_ref[...] = jnp.zeros(s_ref.shape, F32)

    ri = lax.broadcasted_iota(I32, (ch, ch), 0)
    ci = lax.broadcasted_iota(I32, (ch, ch), 1)
    tri = ri >= ci
    strict = ri > ci
    eye = ri == ci
    gbv = gb_ref[...]
    gcum_all = _hdot(jnp.where(tri, 1.0, 0.0), gbv)

    hs = range(n_heads)
    sls = [slice(h * HEAD_DIM, (h + 1) * HEAD_DIM) for h in hs]
    dot = functools.partial(jnp.dot, preferred_element_type=F32)
    q = [qn_ref[:, sl] for sl in sls]
    k = [kn_ref[:, sl] for sl in sls]
    beta = [gbv[:, h:h + 1] for h in hs]
    gc = [gcum_all[:, 8 + h:9 + h] for h in hs]
    gcb = [jnp.broadcast_to(g, (ch, ch)) for g in gc]
    grow = [jnp.sum(jnp.where(eye, g, 0.0), axis=0, keepdims=True) for g in gcb]
    decay = [jnp.where(tri, jnp.exp(jnp.where(tri, g - r, 0.0)), 0.0) for g, r in zip(gcb, grow)]
    kbeta = [kk * b for kk, b in zip(k, beta)]
    egc = [jnp.exp(g) for g in gc]
    k16 = [kk.astype(BF16) for kk in k]
    a = [jnp.where(strict, _nt_dot(kb.astype(BF16), kk) * d, 0.0) for kb, kk, d in zip(kbeta, k16, decay)]
    x = [jnp.concatenate([vv_ref[:, sl] * b, kb * e], axis=1) for sl, b, kb, e in zip(sls, beta, kbeta, egc)]
    npow = [-aa for aa in a]
    span = 1
    while True:
        mm = _dot3 if span <= 2 else _bdot
        x = [xx + mm(n, xx) for n, xx in zip(npow, x)]
        span *= 2
        if span >= ch:
            break
        npow = [mm(n, n) for n in npow]
    s = [s_ref[h] for h in hs]
    s16 = [ss.astype(BF16) for ss in s]
    v_new = [xx[:, :HEAD_DIM] - dot(xx[:, HEAD_DIM:].astype(BF16), ss) for xx, ss in zip(x, s16)]
    v16 = [vn.astype(BF16) for vn in v_new]
    attn = [jnp.where(tri, _nt_dot(qq.astype(BF16), kk) * d, 0.0) for qq, kk, d in zip(q, k16, decay)]
    o = [dot((qq * e).astype(BF16), ss) + dot(at.astype(BF16), vn)
         for qq, e, ss, at, vn in zip(q, egc, s16, attn, v16)]
    g_last = [g[ch - 1:ch, :] for g in gc]
    kdec = [(kk * jnp.exp(gl - g)).astype(BF16) for kk, gl, g in zip(k, g_last, gc)]
    s_new = [ss * jnp.exp(gl) + _tn_dot(kd, vn) for ss, gl, kd, vn in zip(s, g_last, kdec, v16)]
    for h in hs:
        s_ref[h] = s_new[h]
    for h, sl in enumerate(sls):
        on = o[h] * lax.rsqrt(jnp.mean(o[h] * o[h], axis=1, keepdims=True) + NORM_EPS) * nw_ref[...]
        y_ref[:, sl] = (on * _silu(og_ref[:, sl])).astype(y_ref.dtype)

    @pl.when(c == n_chunks - 1)
    def _():
        s_out_ref[0] = s_ref[...]


def _dn_chunks(qn, kn, vv, gb, og, norm_w, batch, seq):
    width = qn.shape[1]
    n_heads = width // HEAD_DIM
    assert seq % DN_CHUNK == 0
    n_chunks = seq // DN_CHUNK
    row_spec = lambda n: pl.BlockSpec((DN_CHUNK, n), lambda b, c: (b * n_chunks + c, 0))
    return pl.pallas_call(
        _dn_chunk_kernel,
        grid=(batch, n_chunks),
        in_specs=[row_spec(width), row_spec(width), row_spec(width), row_spec(HEAD_DIM), row_spec(width),
                  pl.BlockSpec((1, HEAD_DIM), lambda b, c: (0, 0))],
        out_specs=[row_spec(width),
                   pl.BlockSpec((1, n_heads, HEAD_DIM, HEAD_DIM), lambda b, c: (b, 0, 0, 0))],
        out_shape=[jax.ShapeDtypeStruct((batch * seq, width), BF16),
                   jax.ShapeDtypeStruct((batch, n_heads, HEAD_DIM, HEAD_DIM), F32)],
        scratch_shapes=[pltpu.VMEM((n_heads, HEAD_DIM, HEAD_DIM), F32)],
        compiler_params=_params(("parallel", "arbitrary")),
        name="dn_chunks",
    )(qn, kn, vv, gb, og, norm_w)


def _dn_step_kernel(x_ref, cs_ref, bg_ref, og_ref, s_ref, cw_ref, al_ref, dtb_ref, nw_ref,
                    y_ref, s_out_ref, cs_out_ref):
    width3 = x_ref.shape[2]
    width = width3 // 3
    n_heads = width // HEAD_DIM
    xn = x_ref[0]
    cs = cs_ref[0]
    y = xn * cw_ref[CONV_W - 1:CONV_W, :]
    for j in range(CONV_W - 1):
        y = y + cs[j:j + 1, :] * cw_ref[j:j + 1, :]
    for j in range(1, CONV_W - 1):
        cs_out_ref[0, j - 1:j, :] = cs_ref[0, j:j + 1, :]
    cs_out_ref[0, CONV_W - 2:CONV_W - 1, :] = xn
    a = _silu(y)
    qn = _head_l2norm(a[:, :width], n_heads, HEAD_DIM ** -0.5)
    kn = _head_l2norm(a[:, width:2 * width], n_heads, 1.0)
    vv = a[:, 2 * width:]
    gbv = _conv_gate_math(bg_ref[0], al_ref[...], dtb_ref[...])
    og = og_ref[0]
    rows = 16
    rowi = lax.broadcasted_iota(I32, (rows, HEAD_DIM), 0)
    bround = lambda t: t.astype(BF16).astype(F32)
    for h in range(n_heads):
        sl = slice(h * HEAD_DIM, (h + 1) * HEAD_DIM)
        q, k, v = qn[:, sl], kn[:, sl], vv[:, sl]
        beta = gbv[:, h:h + 1]
        eg = jnp.exp(gbv[:, 8 + h:9 + h])
        s = s_ref[0, h]
        lhs = jnp.where(rowi == 0, k * beta * eg, jnp.where(rowi == 1, q * eg, 0.0))
        prod = jnp.dot(lhs.astype(BF16), s.astype(BF16), preferred_element_type=F32)
        v_new = v * beta - prod[0:1, :]
        attn = jnp.sum(bround(q) * bround(k), axis=1, keepdims=True)
        o = prod[1:2, :] + bround(attn) * bround(v_new)
        outer = _tn_dot(jnp.where(rowi == 0, k, 0.0).astype(BF16),
                        jnp.broadcast_to(v_new, (rows, HEAD_DIM)).astype(BF16))
        s_out_ref[0, h] = s * eg + outer
        o = o * lax.rsqrt(jnp.mean(o * o, axis=1, keepdims=True) + NORM_EPS) * nw_ref[...]
        y_ref[0, :, sl] = o * _silu(og[:, sl])


def _dn_step(x3, conv_state, bg3, og3, dn_state, conv_w, al_row, dtb_row, norm_w):
    n_seq, _, width3 = x3.shape
    width = width3 // 3
    n_heads = width // HEAD_DIM
    vec = lambda n: pl.BlockSpec((1, 1, n), lambda s: (s, 0, 0))
    const = lambda r, n: pl.BlockSpec((r, n), lambda s: (0, 0))
    return pl.pallas_call(
        _dn_step_kernel,
        grid=(n_seq,),
        in_specs=[vec(width3),
                  pl.BlockSpec((1, CONV_W - 1, width3), lambda s: (s, 0, 0)),
                  vec(HEAD_DIM), vec(width),
                  pl.BlockSpec((1, n_heads, HEAD_DIM, HEAD_DIM), lambda s: (s, 0, 0, 0)),
                  const(CONV_W, width3), const(1, HEAD_DIM), const(1, HEAD_DIM), const(1, HEAD_DIM)],
        out_specs=[vec(width),
                   pl.BlockSpec((1, n_heads, HEAD_DIM, HEAD_DIM), lambda s: (s, 0, 0, 0)),
                   pl.BlockSpec((1, CONV_W - 1, width3), lambda s: (s, 0, 0))],
        out_shape=[jax.ShapeDtypeStruct((n_seq, 1, width), F32),
                   jax.ShapeDtypeStruct(dn_state.shape, F32),
                   jax.ShapeDtypeStruct(conv_state.shape, F32)],
        compiler_params=_params(("parallel",)),
        name="dn_step",
    )(x3, conv_state, bg3, og3, dn_state, conv_w, al_row, dtb_row, norm_w)


def _top2_sum(a, b, c, d):
    hi1, lo1 = jnp.maximum(a, b), jnp.minimum(a, b)
    hi2, lo2 = jnp.maximum(c, d), jnp.minimum(c, d)
    return jnp.maximum(hi1, hi2) + jnp.maximum(jnp.minimum(hi1, hi2), jnp.maximum(lo1, lo2))


def _outproj_router_kernel(ya_ref, yb_ref, ys_ref, x_ref, w_ref, g_ref, b_ref, wr_ref, rb_ref,
                           x1_ref, info_ref, infot_ref, cnt_ref, carry_ref, mix_ref, *, alpha, n_valid):
    i = pl.program_id(0)
    last = pl.num_programs(0) - 1
    tm = x_ref.shape[0]
    att_w = ya_ref.shape[1]
    dot = functools.partial(jnp.dot, preferred_element_type=F32)

    @pl.when(i == 0)
    def _():
        carry_ref[...] = jnp.zeros(carry_ref.shape, F32)

    @pl.when(i < last)
    def _():
        mix_ref[...] = (dot(ya_ref[...], w_ref[0, :att_w, :].astype(BF16))
                        + dot(yb_ref[...], w_ref[0, att_w:, :].astype(BF16)))

    @pl.when(i == last)
    def _():
        mix_ref[...] = dot(ys_ref[...], w_ref[0].astype(BF16))

    x1 = _layer_norm(alpha * x_ref[...] + mix_ref[...], g_ref[...], b_ref[...])
    x1_ref[...] = x1

    scores = _sigmoid(_nt_dot(wr_ref[...], x1.astype(BF16)))
    selv = scores + rb_ref[:, 0:1]
    sel = [selv[e:e + 1, :] for e in range(N_EXPERTS)]
    sc = [scores[e:e + 1, :] for e in range(N_EXPERTS)]
    n_groups = N_EXPERTS // EXPERTS_PER_GROUP
    gs = [_top2_sum(*sel[EXPERTS_PER_GROUP * g:EXPERTS_PER_GROUP * (g + 1)]) for g in range(n_groups)]
    best, bidx = gs[0], jnp.zeros((1, tm), I32)
    for g in range(1, n_groups):
        better = gs[g] > best
        best = jnp.where(better, gs[g], best)
        bidx = jnp.where(better, g, bidx)
    ninf = jnp.full((1, tm), -jnp.inf, F32)
    vals = [jnp.where(bidx == e // EXPERTS_PER_GROUP, sel[e], ninf) for e in range(N_EXPERTS)]
    b1, i1, s1 = vals[0], jnp.zeros((1, tm), I32), sc[0]
    for e in range(1, N_EXPERTS):
        better = vals[e] > b1
        b1 = jnp.where(better, vals[e], b1)
        i1 = jnp.where(better, e, i1)
        s1 = jnp.where(better, sc[e], s1)
    b2, i2, s2 = ninf, jnp.zeros((1, tm), I32), jnp.zeros((1, tm), F32)
    for e in range(N_EXPERTS):
        cand = jnp.where(i1 == e, ninf, vals[e])
        better = cand > b2
        b2 = jnp.where(better, cand, b2)
        i2 = jnp.where(better, e, i2)
        s2 = jnp.where(better, sc[e], s2)
    tok = i * tm + lax.broadcasted_iota(I32, (1, tm), 1)
    valid = tok < n_valid
    inv = 1.0 / (s1 + s2)
    g1 = jnp.where(valid, s1 * inv, 0.0)
    g2 = jnp.where(valid, s2 * inv, 0.0)

    erow = lax.broadcasted_iota(I32, (N_EXPERTS, tm), 0)
    oh1 = jnp.where((erow == i1) & valid, 1.0, 0.0)
    oh2 = jnp.where((erow == i2) & valid, 1.0, 0.0)
    both = oh1 + oh2
    upper = jnp.where(lax.broadcasted_iota(I32, (tm, tm), 0) <= lax.broadcasted_iota(I32, (tm, tm), 1), 1.0, 0.0)
    incl = jnp.dot(both.astype(BF16), upper.astype(BF16), preferred_element_type=F32)
    before = carry_ref[:, 0:1] + incl - both
    r1 = jnp.sum(oh1 * before, axis=0, keepdims=True)
    r2 = jnp.sum(oh2 * before, axis=0, keepdims=True)
    carry_ref[...] = carry_ref[...] + incl[:, tm - 1:tm]
    cnt_ref[...] = carry_ref[...]

    row8 = lax.broadcasted_iota(I32, (8, tm), 0)
    info = jnp.zeros((8, tm), F32)
    for r, val in enumerate((i1.astype(F32), i2.astype(F32), g1, g2, r1, r2)):
        info = jnp.where(row8 == r, val, info)
    info_ref[...] = info
    infot_ref[...] = jnp.concatenate([info, jnp.zeros((HEAD_DIM - 8, tm), F32)], axis=0).T


def _outproj_router(ya, yb, ys, x, w_out, layer, ln_g, ln_b, wr_t, rb_col, alpha, n_valid):
    nt, d = x.shape
    dm = w_out.shape[1]
    prompt_tiles = ya.shape[0] // ROW_TILE
    assert nt == (prompt_tiles + 1) * ROW_TILE and ys.shape == (ROW_TILE, dm)
    prompt_spec = lambda n: pl.BlockSpec((ROW_TILE, n), lambda i: (jnp.minimum(i, prompt_tiles - 1), 0))
    kern = functools.partial(_outproj_router_kernel, alpha=alpha, n_valid=n_valid)
    return pl.pallas_call(
        kern,
        grid=(nt // ROW_TILE,),
        in_specs=[prompt_spec(ya.shape[1]), prompt_spec(yb.shape[1]),
                  pl.BlockSpec((ROW_TILE, dm), lambda i: (0, 0)),
                  pl.BlockSpec((ROW_TILE, d), lambda i: (i, 0)),
                  pl.BlockSpec((1, dm, d), lambda i: (layer, 0, 0), pipeline_mode=pl.Buffered(1)),
                  pl.BlockSpec((1, d), lambda i: (0, 0)),
                  pl.BlockSpec((1, d), lambda i: (0, 0)),
                  pl.BlockSpec((N_EXPERTS, d), lambda i: (0, 0)),
                  pl.BlockSpec((N_EXPERTS, HEAD_DIM), lambda i: (0, 0))],
        out_specs=[pl.BlockSpec((ROW_TILE, d), lambda i: (i, 0)),
                   pl.BlockSpec((8, ROW_TILE), lambda i: (0, i)),
                   pl.BlockSpec((ROW_TILE, HEAD_DIM), lambda i: (i, 0)),
                   pl.BlockSpec((N_EXPERTS, HEAD_DIM), lambda i: (0, 0))],
        out_shape=[jax.ShapeDtypeStruct((nt, d), F32),
                   jax.ShapeDtypeStruct((8, nt), F32),
                   jax.ShapeDtypeStruct((nt, HEAD_DIM), F32),
                   jax.ShapeDtypeStruct((N_EXPERTS, HEAD_DIM), F32)],
        scratch_shapes=[pltpu.VMEM((N_EXPERTS, HEAD_DIM), F32), pltpu.VMEM((ROW_TILE, d), F32)],
        compiler_params=_params(("arbitrary",)),
        name="outproj_router",
    )(ya, yb, ys, x, w_out, ln_g, ln_b, wr_t, rb_col)


def _row_copy(src_hbm, src_row, dst_ref, dst_row, sem):
    return pltpu.make_async_copy(src_hbm.at[pl.ds(src_row, 1), :], dst_ref.at[pl.ds(dst_row, 1), :], sem)


def _moe_expert_kernel(rt_ref, be_ref, nu_ref, x_hbm, wg_ref, wu_ref, wd_ref, o_ref, xbuf, sems):
    m = pl.program_id(0)
    n_used = nu_ref[0]
    tm = o_ref.shape[0]

    def issue(block, slot):
        def start(r, carry):
            _row_copy(x_hbm, rt_ref[block * tm + r], xbuf.at[slot], r, sems.at[slot]).start()
            return carry

        lax.fori_loop(0, tm, start, 0, unroll=ROW_DMA_UNROLL)

    def drain(slot):
        pltpu.make_async_copy(x_hbm.at[pl.ds(0, tm), :], xbuf.at[slot], sems.at[slot]).wait()

    @pl.when(m == 0)
    def _():
        issue(0, 0)

    @pl.when(m + 1 < n_used)
    def _():
        issue(m + 1, (m + 1) % 2)

    @pl.when(m < n_used)
    def _():
        drain(m % 2)
        xb = xbuf[m % 2].astype(BF16)
        hid = _silu(jnp.dot(xb, wg_ref[0].astype(BF16), preferred_element_type=F32)) * jnp.dot(
            xb, wu_ref[0].astype(BF16), preferred_element_type=F32)
        o_ref[...] = jnp.dot(hid.astype(BF16), wd_ref[0], preferred_element_type=F32)

    @pl.when(m >= n_used)
    def _():
        o_ref[...] = jnp.zeros(o_ref.shape, o_ref.dtype)


def _moe_experts(row_tok, blk_e, n_used, x, wg, wu, wd):
    d = x.shape[1]
    n_rows = row_tok.shape[0]
    f = wg.shape[2]
    return pl.pallas_call(
        _moe_expert_kernel,
        grid_spec=pltpu.PrefetchScalarGridSpec(
            num_scalar_prefetch=3,
            grid=(n_rows // MOE_TILE,),
            in_specs=[pl.BlockSpec(memory_space=pl.ANY),
                      pl.BlockSpec((1, d, f), lambda m, rt, be, nu: (be[m], 0, 0)),
                      pl.BlockSpec((1, d, f), lambda m, rt, be, nu: (be[m], 0, 0)),
                      pl.BlockSpec((1, f, d), lambda m, rt, be, nu: (be[m], 0, 0))],
            out_specs=pl.BlockSpec((MOE_TILE, d), lambda m, rt, be, nu: (m, 0)),
            scratch_shapes=[pltpu.VMEM((2, MOE_TILE, d), F32), pltpu.SemaphoreType.DMA((2,))]),
        out_shape=jax.ShapeDtypeStruct((n_rows, d), F32),
        compiler_params=_params(("arbitrary",)),
        name="moe_experts",
    )(row_tok, blk_e, n_used, x, wg, wu, wd)


def _moe_combine_kernel(dest_ref, y_hbm, x_ref, it_ref, g_ref, b_ref, o_ref, buf_ref, sems, *, alpha, nt):
    i = pl.program_id(0)
    tm = x_ref.shape[0]

    def issue(tile, slot):
        def start(r, carry):
            _row_copy(y_hbm, dest_ref[tile * tm + r], buf_ref.at[slot, 0], r, sems.at[slot]).start()
            _row_copy(y_hbm, dest_ref[nt + tile * tm + r], buf_ref.at[slot, 1], r, sems.at[slot]).start()
            return carry

        lax.fori_loop(0, tm, start, 0, unroll=ROW_DMA_UNROLL)

    def drain(slot):
        for half in range(2):
            pltpu.make_async_copy(y_hbm.at[pl.ds(0, tm), :], buf_ref.at[slot, half], sems.at[slot]).wait()

    @pl.when(i == 0)
    def _():
        issue(0, 0)

    @pl.when(i + 1 < pl.num_programs(0))
    def _():
        issue(i + 1, (i + 1) % 2)

    slot = i % 2
    drain(slot)
    it = it_ref[...]
    f = buf_ref[slot, 0] * it[:, 2:3] + buf_ref[slot, 1] * it[:, 3:4]
    o_ref[...] = _layer_norm(alpha * x_ref[...] + f, g_ref[...], b_ref[...])


def _moe_combine(dest, y_sorted, x1, info_t, ln_g, ln_b, alpha):
    nt, d = x1.shape
    kern = functools.partial(_moe_combine_kernel, alpha=alpha, nt=nt)
    return pl.pallas_call(
        kern,
        grid_spec=pltpu.PrefetchScalarGridSpec(
            num_scalar_prefetch=1,
            grid=(nt // ROW_TILE,),
            in_specs=[pl.BlockSpec(memory_space=pl.ANY),
                      pl.BlockSpec((ROW_TILE, d), lambda i, ds: (i, 0)),
                      pl.BlockSpec((ROW_TILE, HEAD_DIM), lambda i, ds: (i, 0)),
                      pl.BlockSpec((1, d), lambda i, ds: (0, 0)),
                      pl.BlockSpec((1, d), lambda i, ds: (0, 0))],
            out_specs=pl.BlockSpec((ROW_TILE, d), lambda i, ds: (i, 0)),
            scratch_shapes=[pltpu.VMEM((2, 2, ROW_TILE, d), F32), pltpu.SemaphoreType.DMA((2,))]),
        out_shape=jax.ShapeDtypeStruct((nt, d), F32),
        compiler_params=_params(("arbitrary",)),
        name="moe_combine",
    )(dest, y_sorted, x1, info_t, ln_g, ln_b)


def _moe_layer(ya, yb, ys, x, w_out, layer, ln1_g, ln1_b, wr_t, rb_col, wg, wu, wd, ln2_g, ln2_b, alpha, n_valid):
    nt, d = x.shape
    first_expert = layer * N_EXPERTS
    x1, info, info_t, cnt = _outproj_router(ya, yb, ys, x, w_out, layer, ln1_g, ln1_b, wr_t, rb_col, alpha, n_valid)
    n_blocks = -(-(2 * n_valid) // MOE_TILE) + N_EXPERTS
    counts = cnt[:, 0].astype(I32)
    padded = (counts + MOE_TILE - 1) // MOE_TILE * MOE_TILE
    pad_end = jnp.cumsum(padded)
    pad_start = pad_end - padded
    e1, e2 = info[0].astype(I32), info[1].astype(I32)
    dest1 = pad_start[e1] + info[4].astype(I32)
    dest2 = pad_start[e2] + info[5].astype(I32)
    tok = jnp.arange(n_valid, dtype=I32)
    row_tok = jnp.zeros((n_blocks * MOE_TILE,), I32).at[dest1[:n_valid]].set(tok).at[dest2[:n_valid]].set(tok)
    n_used = (pad_end[-1] // MOE_TILE).astype(I32)
    blk_start = jnp.arange(n_blocks, dtype=I32) * MOE_TILE
    blk_e = jnp.minimum(jnp.sum(pad_end[None, :] <= blk_start[:, None], axis=1), N_EXPERTS - 1).astype(I32)
    blk_e = jnp.where(jnp.arange(n_blocks) < n_used, blk_e, blk_e[jnp.maximum(n_used - 1, 0)])
    valid = jnp.arange(nt) < n_valid
    dest = jnp.concatenate([jnp.where(valid, dest1, 0), jnp.where(valid, dest2, 0)]).astype(I32)
    nu = n_used.reshape(1)
    ys = _moe_experts(row_tok, blk_e + first_expert, nu, x1, wg, wu, wd)
    return _moe_combine(dest, ys, x1, info_t, ln2_g, ln2_b, alpha)


def kernel(x_prompt, x_sample, cache_k, cache_v, state_dn, state_conv, page_table, ln_in_g, ln_in_b, rel_bias,
           w_router, router_bias, w_in, w_out, conv_w, a_log, dt_bias, dn_norm_w, ln1_g, ln1_b, ln2_g, ln2_b,
           w_gate, w_up, w_down):
    batch, seq, d = x_prompt.shape
    n_seq, dec_seq, _ = x_sample.shape
    depth = w_in.shape[0]
    n_pool, page = cache_k.shape[1], cache_k.shape[2]
    kvh = cache_k.shape[3]
    n_pages = page_table.shape[1]
    past_len = n_pages * page
    assert dec_seq == 1 and past_len % MOBA_BLOCK == 0 and n_seq <= ROW_TILE
    att_w = d // 2
    dn_w = d - att_w
    heads = att_w // HEAD_DIM
    dn_heads = dn_w // HEAD_DIM
    kv_w = kvh * HEAD_DIM
    assert dn_heads == 8 and heads == ATT_GROUP * kvh
    alpha = (2 * depth) ** 0.25
    n_prompt = batch * seq
    nt = n_prompt + ROW_TILE
    n_valid = n_prompt + n_seq

    x = jnp.concatenate([x_prompt.reshape(n_prompt, d), x_sample.reshape(n_seq, d),
                         jnp.zeros((ROW_TILE - n_seq, d), F32)], axis=0)
    x = _input_layer_norm(x, ln_in_g, ln_in_b)

    widths = (att_w, kv_w, kv_w, 3 * dn_w, dn_w, HEAD_DIM)
    in_cols = w_in.shape[2]
    w_in_b = jnp.pad(w_in, ((0, 0), (0, 0), (0, sum(widths) - in_cols))).astype(BF16)
    all_experts = lambda w, dt: w.astype(dt).reshape((depth * N_EXPERTS,) + w.shape[2:])
    wg_all, wu_all, wd_all = all_experts(w_gate, F32), all_experts(w_up, F32), all_experts(w_down, BF16)
    wr_t = w_router.T.astype(BF16)
    rb_col = jnp.broadcast_to(router_bias.astype(F32)[:, None], (N_EXPERTS, HEAD_DIM))
    lane_pad = lambda v: jnp.pad(v.astype(F32), ((0, 0), (dn_heads, HEAD_DIM - 2 * dn_heads)))
    al_rows, dtb_rows = lane_pad(a_log), lane_pad(dt_bias)
    bias_tiles = _bias_tiles(rel_bias, seq // MOBA_BLOCK)
    pt = page_table.astype(I32)

    ks_p, vs_p, dn_p, cv_p, ks_s, vs_s, dn_s, cv_s = [], [], [], [], [], [], [], []
    for l in range(depth):
        qa, ka, va, qkvb, og, bg = _in_projection(x, w_in_b, l, widths)
        al_row, dtb_row = al_rows[l:l + 1], dtb_rows[l:l + 1]
        norm_w = dn_norm_w[l].reshape(1, HEAD_DIM).astype(F32)

        ya_p = _moba_prompt(qa, ka, va, bias_tiles, batch, seq)
        qn, kn, vv, gb = _dn_prep(qkvb, bg, conv_w[l], al_row, dtb_row, batch, seq)
        yb_p, s_p = _dn_chunks(qn, kn, vv, gb, og, norm_w, batch, seq)

        sl = slice(n_prompt, n_prompt + n_seq)
        q3 = qa[sl].reshape(n_seq, heads, HEAD_DIM)
        kn3 = ka[sl].reshape(n_seq, kvh, HEAD_DIM)
        vn3 = va[sl].reshape(n_seq, kvh, HEAD_DIM)
        ksum = _past_block_sums(cache_k, l, pt)
        top = _decode_select(q3, ksum.transpose(0, 2, 1, 3))[:, :, :MOBA_TOPK]
        ya_s = _decode_attend(q3, kn3, vn3, cache_k, cache_v, l, pt, top, rel_bias, past_len)
        yb_s, s_s, c_s = _dn_step(qkvb[sl].reshape(n_seq, 1, 3 * dn_w), state_conv[l],
                                  bg[sl].reshape(n_seq, 1, HEAD_DIM), og[sl].reshape(n_seq, 1, dn_w),
                                  state_dn[l], conv_w[l], al_row, dtb_row, norm_w)

        y_s = jnp.concatenate([ya_s.reshape(n_seq, att_w), yb_s.reshape(n_seq, dn_w)], axis=1).astype(BF16)
        y_s = jnp.pad(y_s, ((0, ROW_TILE - n_seq), (0, 0)))
        x = _moe_layer(ya_p, yb_p, y_s, x, w_out.astype(F32), l, ln1_g[l].reshape(1, d), ln1_b[l].reshape(1, d),
                       wr_t, rb_col, wg_all, wu_all, wd_all, ln2_g[l].reshape(1, d), ln2_b[l].reshape(1, d),
                       alpha, n_valid)

        ks_p.append(ka)
        vs_p.append(va)
        dn_p.append(s_p)
        cv_p.append(qkvb[:n_prompt].reshape(batch, seq, 3 * dn_w)[:, seq - (CONV_W - 1):])
        ks_s.append(kn3.reshape(n_seq, 1, kvh, HEAD_DIM))
        vs_s.append(vn3.reshape(n_seq, 1, kvh, HEAD_DIM))
        dn_s.append(s_s)
        cv_s.append(c_s)

    y_prompt = x[:n_prompt].reshape(batch, seq, d)
    y_sample = x[n_prompt:n_prompt + n_seq].reshape(n_seq, 1, d)
    prompt_kv = lambda rows: jnp.stack([r[:n_prompt] for r in rows]).reshape(depth, batch, seq, kvh, HEAD_DIM)
    return (y_prompt, y_sample, prompt_kv(ks_p), prompt_kv(vs_p), jnp.stack(dn_p), jnp.stack(cv_p),
            jnp.stack(ks_s), jnp.stack(vs_s), jnp.stack(dn_s), jnp.stack(cv_s))
```
